```python
import math
import jax
import jax.numpy as jnp
from jax import lax
import numpy as np

D_MODEL = 1024
BATCH = 4
SEQ = 8192
DEPTH = 4

N_EVEN = (DEPTH + 1) // 2
N_ODD = DEPTH // 2

DIFF_HEADS = 4
DIFF_QK_DIM = 64
DIFF_V_DIM = 2 * DIFF_QK_DIM
ATTN_Q_BLOCK = 128
MOBA_HEADS = 4
MOBA_HEAD_DIM = 128
MOBA_BLOCK = 256
MOBA_TOPK = 3
MOBA_Q_CHUNK = 64
DIFF_QK_W = DIFF_HEADS * 2 * DIFF_QK_DIM
DIFF_V_W = DIFF_HEADS * DIFF_V_DIM
MOBA_W = MOBA_HEADS * MOBA_HEAD_DIM
HY_IN_W = 2 * DIFF_QK_W + DIFF_V_W + 3 * MOBA_W
HY_MIX_W = DIFF_V_W + MOBA_W
GLA_HEADS = 4
GLA_K_W = D_MODEL // 2
GLA_V_W = D_MODEL
GLA_DK = GLA_K_W // GLA_HEADS
GLA_DV = GLA_V_W // GLA_HEADS
GLA_GATE_RANK = 16
GLA_GATE_NORM = 16.0
GLA_CHUNK = 64
GLA_IN_W = 2 * GLA_K_W + 2 * GLA_V_W + GLA_GATE_RANK
D_FF = 4 * D_MODEL
ROPE_THETA = 10000.0
LN_EPS = 1e-5
RMS_EPS = 1e-5
DEEPNORM_ALPHA = (2 * DEPTH) ** 0.25
DEEPNORM_BETA = (8 * DEPTH) ** -0.25

kernel_name = "hybrid_diff_moba_gla_deepnorm"


def layer_norm(x, g, b):
    xf = x.astype(jnp.float32)
    mu = jnp.mean(xf, axis=-1, keepdims=True)
    var = jnp.mean(jnp.square(xf - mu), axis=-1, keepdims=True)
    return (((xf - mu) * lax.rsqrt(var + LN_EPS)) * g + b).astype(x.dtype)


def rms_norm(x, g):
    xf = x.astype(jnp.float32)
    y = xf * lax.rsqrt(jnp.mean(jnp.square(xf), axis=-1, keepdims=True) + RMS_EPS)
    return (y * g).astype(x.dtype)


def rope_tables(seq, dim):
    inv = 1.0 / (ROPE_THETA ** (jnp.arange(0, dim, 2, dtype=jnp.float32) / dim))
    ang = jnp.arange(seq, dtype=jnp.float32)[:, None] * inv[None, :]
    return jnp.cos(ang), jnp.sin(ang)


def apply_rope(x, cos, sin):
    x1, x2 = jnp.split(x, 2, axis=-1)
    c = cos.astype(x.dtype)
    s = sin.astype(x.dtype)
    return jnp.concatenate([x1 * c - x2 * s, x2 * c + x1 * s], axis=-1)


def diff_attention(q, k, v, lam, subln, lambda_init):
    Bn, H, _, S, dq = q.shape
    scale = dq ** -0.5
    lf = lam.astype(jnp.float32)
    lam_full = jnp.exp(jnp.sum(lf[0] * lf[1])) - jnp.exp(jnp.sum(lf[2] * lf[3])) + lambda_init
    nq = S // ATTN_Q_BLOCK
    qb = q.reshape(Bn, H, 2, nq, ATTN_Q_BLOCK, dq).transpose(3, 0, 1, 2, 4, 5)
    kpos = jnp.arange(S)

    def block(args):
        qi, i = args
        s = jnp.einsum('bhmqd,bhmkd->bhmqk', qi, k).astype(jnp.float32) * scale
        qpos = i * ATTN_Q_BLOCK + jnp.arange(ATTN_Q_BLOCK)
        s = jnp.where(kpos[None, :] <= qpos[:, None], s, -jnp.inf)
        p = jax.nn.softmax(s, axis=-1)
        w = p[:, :, 0] - lam_full * p[:, :, 1]
        return jnp.einsum('bhqk,bhkd->bhqd', w.astype(v.dtype), v)

    o = lax.map(block, (qb, jnp.arange(nq)))
    o = o.transpose(1, 2, 0, 3, 4).reshape(Bn, H, S, v.shape[-1])
    return rms_norm(o, subln) * (1.0 - lambda_init)


def moba_attention(q, k, v):
    Bn, H, S, d = q.shape
    scale = d ** -0.5
    s_pad = -(-S // MOBA_BLOCK) * MOBA_BLOCK
    pad = ((0, 0), (0, 0), (0, s_pad - S), (0, 0))
    q, k, v = jnp.pad(q, pad), jnp.pad(k, pad), jnp.pad(v, pad)
    nb = s_pad // MOBA_BLOCK
    topk = min(MOBA_TOPK, nb)
    kb = k.reshape(Bn, H, nb, MOBA_BLOCK, d)
    vb = v.reshape(Bn, H, nb, MOBA_BLOCK, d)
    kmean = jnp.mean(kb, axis=3)
    nc = s_pad // MOBA_Q_CHUNK
    qc = q.reshape(Bn, H, nc, MOBA_Q_CHUNK, d).transpose(2, 0, 1, 3, 4)
    gather = jax.vmap(jax.vmap(lambda blocks, idx: blocks[idx]))
    blk_ids = jnp.arange(nb)

    def chunk(args):
        qi, c = args
        start = c * MOBA_Q_CHUNK
        own = start // MOBA_BLOCK
        qpos = start + jnp.arange(MOBA_Q_CHUNK)
        gate = jnp.einsum('bhqd,bhnd->bhqn', qi, kmean).astype(jnp.float32)
        gate = jnp.where(blk_ids < own, gate, -jnp.inf)
        _, sel = lax.top_k(gate, topk)
        sel_valid = sel < own
        ks = gather(kb, sel)
        vs = gather(vb, sel)
        s_sel = jnp.einsum('bhqd,bhqnld->bhqnl', qi, ks).astype(jnp.float32) * scale
        s_sel = jnp.where(sel_valid[..., None], s_sel, -jnp.inf).reshape(Bn, H, MOBA_Q_CHUNK, topk * MOBA_BLOCK)
        k_own = lax.dynamic_index_in_dim(kb, own, axis=2, keepdims=False)
        v_own = lax.dynamic_index_in_dim(vb, own, axis=2, keepdims=False)
        s_own = jnp.einsum('bhqd,bhld->bhql', qi, k_own).astype(jnp.float32) * scale
        kpos = own * MOBA_BLOCK + jnp.arange(MOBA_BLOCK)
        s_own = jnp.where(kpos[None, :] <= qpos[:, None], s_own, -jnp.inf)
        p = jax.nn.softmax(jnp.concatenate([s_sel, s_own], axis=-1), axis=-1).astype(v.dtype)
        p_sel = p[..., :topk * MOBA_BLOCK].reshape(Bn, H, MOBA_Q_CHUNK, topk, MOBA_BLOCK)
        p_own = p[..., topk * MOBA_BLOCK:]
        return (jnp.einsum('bhqnl,bhqnld->bhqd', p_sel, vs)
                + jnp.einsum('bhql,bhld->bhqd', p_own, v_own))

    o = lax.map(chunk, (qc, jnp.arange(nc)))
    o = o.transpose(1, 2, 0, 3, 4).reshape(Bn, H, s_pad, d)
    return o[:, :, :S]


def gla_chunked(q, k, v, g):
    dtype = v.dtype
    Bn, H, S, dk = q.shape
    dv = v.shape[-1]
    L = GLA_CHUNK
    nc = S // L
    f32 = jnp.float32

    def to_chunks(t):
        return t.astype(f32).reshape(Bn, H, nc, L, t.shape[-1]).transpose(2, 0, 1, 3, 4)

    qc, kc, vc, gc = to_chunks(q * dk ** -0.5), to_chunks(k), to_chunks(v), to_chunks(g)
    causal = jnp.tril(jnp.ones((L, L), dtype=bool))[None, None, :, :, None]

    def step(state, inp):
        qi, ki, vi, gi = inp
        b = jnp.cumsum(gi, axis=2)
        o_inter = jnp.einsum('bhlk,bhkv->bhlv', qi * jnp.exp(b), state)
        rel = jnp.where(causal, b[:, :, :, None, :] - b[:, :, None, :, :], -jnp.inf)
        att = jnp.einsum('bhik,bhijk,bhjk->bhij', qi, jnp.exp(rel), ki)
        o_intra = jnp.einsum('bhij,bhjv->bhiv', att, vi)
        b_last = b[:, :, -1:, :]
        state = (jnp.exp(b_last[:, :, 0, :, None]) * state
                 + jnp.einsum('bhjk,bhjv->bhkv', ki * jnp.exp(b_last - b), vi))
        return state, o_inter + o_intra

    state0 = jnp.zeros((Bn, H, dk, dv), f32)
    _, o = lax.scan(step, state0, (qc, kc, vc, gc))
    return o.transpose(1, 2, 0, 3, 4).reshape(Bn, H, S, dv).astype(dtype)


def split_heads(t, n, d):
    Bn, S, _ = t.shape
    return t.reshape(Bn, S, n, d).transpose(0, 2, 1, 3)


def merge_heads(t):
    Bn, H, S, d = t.shape
    return t.transpose(0, 2, 1, 3).reshape(Bn, S, H * d)


def diff_moba_mixer(x, w_in, lam, subln, w_out, lambda_init, rope_d, rope_m):
    Bn, S, _ = x.shape
    h = x @ w_in
    o1 = DIFF_QK_W
    o2 = o1 + DIFF_QK_W
    o3 = o2 + DIFF_V_W
    o4 = o3 + MOBA_W
    o5 = o4 + MOBA_W

    def two_maps(t):
        return t.reshape(Bn, S, DIFF_HEADS, 2, DIFF_QK_DIM).transpose(0, 2, 3, 1, 4)

    dq = apply_rope(two_maps(h[..., :o1]), *rope_d)
    dk = apply_rope(two_maps(h[..., o1:o2]), *rope_d)
    dv = split_heads(h[..., o2:o3], DIFF_HEADS, DIFF_V_DIM)
    mq = apply_rope(split_heads(h[..., o3:o4], MOBA_HEADS, MOBA_HEAD_DIM), *rope_m)
    mk = apply_rope(split_heads(h[..., o4:o5], MOBA_HEADS, MOBA_HEAD_DIM), *rope_m)
    mv = split_heads(h[..., o5:], MOBA_HEADS, MOBA_HEAD_DIM)
    a = diff_attention(dq, dk, dv, lam, subln, lambda_init)
    b = moba_attention(mq, mk, mv)
    return jnp.concatenate([merge_heads(a), merge_heads(b)], axis=-1) @ w_out


def gla_mixer(x, w_in, w_gate_up, b_gate, norm_g, w_out):
    h = x @ w_in
    o1 = GLA_K_W
    o2 = o1 + GLA_K_W
    o3 = o2 + GLA_V_W
    o4 = o3 + GLA_V_W
    q = split_heads(h[..., :o1], GLA_HEADS, GLA_DK)
    k = split_heads(h[..., o1:o2], GLA_HEADS, GLA_DK)
    v = split_heads(h[..., o2:o3], GLA_HEADS, GLA_DV)
    r = h[..., o3:o4]
    g = jax.nn.log_sigmoid((h[..., o4:] @ w_gate_up + b_gate).astype(jnp.float32)) / GLA_GATE_NORM
    g = split_heads(g, GLA_HEADS, GLA_DK)
    o = rms_norm(gla_chunked(q, k, v, g), norm_g)
    return (merge_heads(o) * jax.nn.silu(r)) @ w_out


def sqrelu_mlp(x, w1, w2):
    return jnp.square(jax.nn.relu(x @ w1)) @ w2


def setup_inputs(seed: int = 0) -> dict:
    key = jax.random.key(seed)
    ks = jax.random.split(key, 17)

    def normal(k, shape, scale):
        return jax.random.normal(k, shape, jnp.float32) * scale

    return {
        'x': normal(ks[0], (BATCH, SEQ, D_MODEL), 1.0),
        'hy_w_in': normal(ks[1], (N_EVEN, D_MODEL, HY_IN_W), D_MODEL ** -0.5),
        'diff_lambda': normal(ks[2], (N_EVEN, 4, DIFF_QK_DIM), 0.1),
        'diff_subln': 1.0 + normal(ks[3], (N_EVEN, DIFF_V_DIM), 0.02),
        'hy_w_out': normal(ks[4], (N_EVEN, HY_MIX_W, D_MODEL), HY_MIX_W ** -0.5 * DEEPNORM_BETA),
        'gla_w_in': normal(ks[5], (N_ODD, D_MODEL, GLA_IN_W), D_MODEL ** -0.5),
        'gla_w_gate_up': normal(ks[6], (N_ODD, GLA_GATE_RANK, GLA_K_W), GLA_GATE_RANK ** -0.5),
        'gla_b_gate': normal(ks[7], (N_ODD, GLA_K_W), 0.02),
        'gla_norm': 1.0 + normal(ks[8], (N_ODD, GLA_DV), 0.02),
        'gla_w_out': normal(ks[9], (N_ODD, GLA_V_W, D_MODEL), GLA_V_W ** -0.5 * DEEPNORM_BETA),
        'ln_mix_g': 1.0 + normal(ks[10], (DEPTH, D_MODEL), 0.02),
        'ln_mix_b': normal(ks[11], (DEPTH, D_MODEL), 0.02),
        'ffn_w1': normal(ks[12], (DEPTH, D_MODEL, D_FF), D_MODEL ** -0.5),
        'ffn_w2': normal(ks[13], (DEPTH, D_FF, D_MODEL), D_FF ** -0.5 * DEEPNORM_BETA),
        'ln_ffn_g': 1.0 + normal(ks[14], (DEPTH, D_MODEL), 0.02),
        'ln_ffn_b': normal(ks[15], (DEPTH, D_MODEL), 0.02),
    }


def reference(x, hy_w_in, diff_lambda, diff_subln, hy_w_out, gla_w_in, gla_w_gate_up,
              gla_b_gate, gla_norm, gla_w_out, ln_mix_g, ln_mix_b, ffn_w1, ffn_w2,
              ln_ffn_g, ln_ffn_b):
    S = x.shape[1]
    rope_d = rope_tables(S, DIFF_QK_DIM)
    rope_m = rope_tables(S, MOBA_HEAD_DIM)
    for l in range(DEPTH):
        if l % 2 == 0:
            e = l // 2
            lambda_init = 0.8 - 0.6 * math.exp(-0.3 * l)
            mix = diff_moba_mixer(x, hy_w_in[e], diff_lambda[e], diff_subln[e], hy_w_out[e],
                                  lambda_init, rope_d, rope_m)
        else:
            o = l // 2
            mix = gla_mixer(x, gla_w_in[o], gla_w_gate_up[o], gla_b_gate[o], gla_norm[o], gla_w_out[o])
        x = layer_norm(DEEPNORM_ALPHA * x + mix, ln_mix_g[l], ln_mix_b[l])
        x = layer_norm(DEEPNORM_ALPHA * x + sqrelu_mlp(x, ffn_w1[l], ffn_w2[l]), ln_ffn_g[l], ln_ffn_b[l])
    return x
```

```python
import functools
import math

import jax
import jax.numpy as jnp
import numpy as np
from jax import lax
from jax.experimental import pallas as pl
from jax.experimental.pallas import tpu as pltpu

F32 = jnp.float32
BF16 = jnp.bfloat16

DIFF_HEADS = 4
DIFF_QK_DIM = 64
DIFF_V_DIM = 128
MOBA_HEADS = 4
MOBA_HEAD_DIM = 128
MOBA_BLOCK = 256
MOBA_TOPK = 3
GLA_HEADS = 4
GLA_DK = 128
GLA_DV = 256
GLA_GATE_RANK = 16
GLA_GATE_NORM = 16.0
ROPE_THETA = 10000.0
LN_EPS = 1e-5
RMS_EPS = 1e-5

LANES = 128
VMEM_LIMIT_BYTES = 56 * 1024 * 1024

ROW_TILE = 512
COL_CHUNK = 512
ATTN_TILE = 512
GLA_CHUNK = 64
GLA_SUB = 16
GLA_STEP = 256
NEG_BIG = -1e30


def _nt_dot(a, b, precision=None):
    return lax.dot_general(a, b, (((1,), (1,)), ((), ())),
                           preferred_element_type=F32, precision=precision)


def _tn_dot(a, b, precision=None):
    return lax.dot_general(a, b, (((0,), (0,)), ((), ())),
                           preferred_element_type=F32, precision=precision)


def _const_spec(shape):
    nd = len(shape)
    return pl.BlockSpec(shape, lambda *_: (0,) * nd, pipeline_mode=pl.Buffered(1))


def _params(*sem):
    return pltpu.CompilerParams(dimension_semantics=sem, vmem_limit_bytes=VMEM_LIMIT_BYTES)


def _layer_norm_rows(y, g, b):
    mu = jnp.mean(y, axis=-1, keepdims=True)
    d = y - mu
    var = jnp.mean(d * d, axis=-1, keepdims=True)
    return d * lax.rsqrt(var + LN_EPS) * g + b


def _rope_tables(seq, group):
    half = group // 2
    inv = 1.0 / (ROPE_THETA ** (jnp.arange(0, group, 2, dtype=F32) / group))
    ang = jnp.arange(seq, dtype=F32)[:, None] * inv[None, :]
    cos, sin = jnp.cos(ang), jnp.sin(ang)
    cos_g = jnp.concatenate([cos, cos], axis=-1)
    sin_g = jnp.concatenate([-sin, sin], axis=-1)
    reps = LANES // group
    return jnp.tile(cos_g, (1, reps)), jnp.tile(sin_g, (1, reps))


def _rope(acc, cos, sin, group):
    width = acc.shape[1]
    half = group // 2
    lane = lax.broadcasted_iota(jnp.int32, acc.shape, 1)
    upper = pltpu.roll(acc, width - half, axis=1)
    lower = pltpu.roll(acc, half, axis=1)
    partner = jnp.where((lane % group) < half, upper, lower)
    reps = width // LANES
    c = jnp.concatenate([cos] * reps, axis=1)
    s = jnp.concatenate([sin] * reps, axis=1)
    return acc * c + partner * s


def _hy_inproj_kernel(x_ref, w_ref, cd_ref, sd_ref, cm_ref, sm_ref, h_ref, km_ref):
    xb = x_ref[...].astype(BF16)
    rows = xb.shape[0]
    n_chunks = w_ref.shape[1] // COL_CHUNK
    for c in range(n_chunks):
        cols = slice(c * COL_CHUNK, (c + 1) * COL_CHUNK)
        acc = jnp.dot(xb, w_ref[:, cols], preferred_element_type=F32)
        if c == 0:
            acc = _rope(acc, cd_ref[...], sd_ref[...], DIFF_QK_DIM) * (DIFF_QK_DIM ** -0.5)
        elif c == 1:
            acc = _rope(acc, cd_ref[...], sd_ref[...], DIFF_QK_DIM)
        elif c in (3, 4):
            acc = _rope(acc, cm_ref[...], sm_ref[...], MOBA_HEAD_DIM)
        if c == 4:
            blocks = rows // MOBA_BLOCK
            km_ref[0] = jnp.mean(acc.reshape(blocks, MOBA_BLOCK, COL_CHUNK), axis=1)
        h_ref[:, cols] = acc.astype(BF16)


def _hy_inproj(x2, w, tabs, seq):
    tokens, d = x2.shape
    width = w.shape[1]
    assert width == 6 * COL_CHUNK and seq % ROW_TILE == 0 and ROW_TILE % MOBA_BLOCK == 0
    steps = tokens // ROW_TILE
    per_seq = seq // ROW_TILE
    blocks = ROW_TILE // MOBA_BLOCK
    tab_spec = pl.BlockSpec((ROW_TILE, LANES), lambda i: (i % per_seq, 0))
    return pl.pallas_call(
        _hy_inproj_kernel,
        grid=(steps,),
        in_specs=[pl.BlockSpec((ROW_TILE, d), lambda i: (i, 0)),
                  _const_spec((d, width)),
                  tab_spec, tab_spec, tab_spec, tab_spec],
        out_specs=[pl.BlockSpec((ROW_TILE, width), lambda i: (i, 0)),
                   pl.BlockSpec((1, blocks, COL_CHUNK), lambda i: (i, 0, 0))],
        out_shape=[jax.ShapeDtypeStruct((tokens, width), BF16),
                   jax.ShapeDtypeStruct((steps, blocks, COL_CHUNK), F32)],
        compiler_params=_params("arbitrary"),
        name="hy_inproj",
    )(x2, w, *tabs)


def _causal_pairs(n):
    qi = np.array([q for q in range(n) for _ in range(q + 1)], np.int32)
    ki = np.array([k for q in range(n) for k in range(q + 1)], np.int32)
    return jnp.asarray(qi), jnp.asarray(ki)


def _online_softmax_step(s, v, m_scr, l_scr, acc_scr):
    m_prev = m_scr[...]
    m_new = jnp.maximum(m_prev, jnp.max(s, axis=-1, keepdims=True))
    alpha = jnp.exp(m_prev - m_new)
    p = jnp.exp(s - m_new)
    l_scr[...] = alpha * l_scr[...] + jnp.sum(p, axis=-1, keepdims=True)
    acc_scr[...] = alpha * acc_scr[...] + jnp.dot(p.astype(BF16), v, preferred_element_type=F32)
    m_scr[...] = m_new


def _diff_attn_kernel(qi_tab, ki_tab, q_ref, k_ref, v_ref, lam_ref, sub_ref, o_ref,
                      qs_scr, m_scr, l_scr, acc_scr, *, lambda_init):
    p_id = pl.program_id(2)
    qi = qi_tab[p_id]
    ki = ki_tab[p_id]
    tq = q_ref.shape[1]
    tk = k_ref.shape[1]

    @pl.when(ki == 0)
    def _init():
        q = q_ref[0]
        lane = lax.broadcasted_iota(jnp.int32, q.shape, 1)
        zero = jnp.zeros_like(q)
        qs_scr[0:tq, :] = jnp.where(lane < DIFF_QK_DIM, q, zero)
        qs_scr[tq:2 * tq, :] = jnp.where(lane >= DIFF_QK_DIM, q, zero)
        m_scr[...] = jnp.full(m_scr.shape, NEG_BIG, F32)
        l_scr[...] = jnp.zeros(l_scr.shape, F32)
        acc_scr[...] = jnp.zeros(acc_scr.shape, F32)

    def step(masked):
        s = _nt_dot(qs_scr[...], k_ref[0])
        if masked:
            row = lax.broadcasted_iota(jnp.int32, s.shape, 0) % tq
            col = lax.broadcasted_iota(jnp.int32, s.shape, 1)
            s = jnp.where(col <= row, s, NEG_BIG)
        _online_softmax_step(s, v_ref[0], m_scr, l_scr, acc_scr)

    @pl.when(ki < qi)
    def _full():
        step(False)

    @pl.when(ki == qi)
    def _diag():
        step(True)
        lam = lam_ref[...].astype(F32)
        lam_full = (jnp.exp(jnp.sum(lam[0:1] * lam[1:2], axis=-1, keepdims=True))
                    - jnp.exp(jnp.sum(lam[2:3] * lam[3:4], axis=-1, keepdims=True))
                    + lambda_init)
        o1 = acc_scr[0:tq, :] / l_scr[0:tq, :]
        o2 = acc_scr[tq:2 * tq, :] / l_scr[tq:2 * tq, :]
        o = o1 - lam_full * o2
        y = o * lax.rsqrt(jnp.mean(o * o, axis=-1, keepdims=True) + RMS_EPS)
        o_ref[0] = (y * sub_ref[...] * (1.0 - lambda_init)).astype(o_ref.dtype)


def _diff_attention(h3, lam, subln, lambda_init):
    bsz, seq, _ = h3.shape
    t = ATTN_TILE
    assert seq % t == 0
    n = seq // t
    qi_tab, ki_tab = _causal_pairs(n)
    hb = DIFF_HEADS
    grid_spec = pltpu.PrefetchScalarGridSpec(
        num_scalar_prefetch=2,
        grid=(bsz, DIFF_HEADS, int(qi_tab.shape[0])),
        in_specs=[
            pl.BlockSpec((1, t, LANES), lambda b, h, p, qt, kt: (b, qt[p], h)),
            pl.BlockSpec((1, t, LANES), lambda b, h, p, qt, kt: (b, kt[p], hb + h)),
            pl.BlockSpec((1, t, LANES), lambda b, h, p, qt, kt: (b, kt[p], 2 * hb + h)),
            pl.BlockSpec((4, DIFF_QK_DIM), lambda b, h, p, qt, kt: (0, 0)),
            pl.BlockSpec((1, DIFF_V_DIM), lambda b, h, p, qt, kt: (0, 0)),
        ],
        out_specs=pl.BlockSpec((1, t, LANES), lambda b, h, p, qt, kt: (b, qt[p], h)),
        scratch_shapes=[pltpu.VMEM((2 * t, LANES), BF16),
                        pltpu.VMEM((2 * t, 1), F32),
                        pltpu.VMEM((2 * t, 1), F32),
                        pltpu.VMEM((2 * t, DIFF_V_DIM), F32)],
    )
    return pl.pallas_call(
        functools.partial(_diff_attn_kernel, lambda_init=lambda_init),
        grid_spec=grid_spec,
        out_shape=jax.ShapeDtypeStruct((bsz, seq, DIFF_HEADS * DIFF_V_DIM), BF16),
        compiler_params=_params("arbitrary", "arbitrary", "arbitrary"),
        name="diff_attn",
    )(qi_tab, ki_tab, h3, h3, h3, lam, subln.reshape(1, DIFF_V_DIM))


def _moba_kernel(qi_tab, ki_tab, q_ref, k_ref, v_ref, km_ref, o_ref,
                 sel_scr, m_scr, l_scr, acc_scr, *, scale, topk):
    p_id = pl.program_id(2)
    qi = qi_tab[p_id]
    ki = ki_tab[p_id]
    tq = q_ref.shape[1]
    tk = k_ref.shape[1]
    nb = km_ref.shape[1]
    per_tile = tk // MOBA_BLOCK

    def own_block():
        row = lax.broadcasted_iota(jnp.int32, (tq, 1), 0)
        return (qi * tq + row) // MOBA_BLOCK

    @pl.when(ki == 0)
    def _init():
        gate = _nt_dot(q_ref[0].astype(F32), km_ref[0], precision=lax.Precision.HIGHEST)
        blk = lax.broadcasted_iota(jnp.int32, gate.shape, 1)
        own = own_block()
        g = jnp.where(blk < own, gate, -jnp.inf)
        sel = jnp.zeros(gate.shape, F32)
        for _ in range(topk):
            mx = jnp.max(g, axis=-1, keepdims=True)
            idx = jnp.min(jnp.where(g == mx, blk, nb), axis=-1, keepdims=True)
            hit = (blk == idx) & (mx > -jnp.inf)
            sel = jnp.where(hit, 1.0, sel)
            g = jnp.where(blk == idx, -jnp.inf, g)
        sel_scr[...] = jnp.where(blk == own, 1.0, sel)
        m_scr[...] = jnp.full(m_scr.shape, NEG_BIG, F32)
        l_scr[...] = jnp.zeros(l_scr.shape, F32)
        acc_scr[...] = jnp.zeros(acc_scr.shape, F32)

    def step(diagonal):
        s = _nt_dot(q_ref[0], k_ref[0]) * scale
        sel = sel_scr[...]
        blk = lax.broadcasted_iota(jnp.int32, sel.shape, 1)
        col = lax.broadcasted_iota(jnp.int32, s.shape, 1)
        picked = jnp.zeros(s.shape, F32)
        for j in range(per_tile):
            cj = jnp.sum(jnp.where(blk == ki * per_tile + j, sel, 0.0), axis=-1, keepdims=True)
            picked = jnp.where(col // MOBA_BLOCK == j, cj, picked)
        allowed = picked > 0.0
        if diagonal:
            row = lax.broadcasted_iota(jnp.int32, s.shape, 0)
            allowed = allowed & (col <= row)
        s = jnp.where(allowed, s, NEG_BIG)
        _online_softmax_step(s, v_ref[0], m_scr, l_scr, acc_scr)

    @pl.when(ki < qi)
    def _past():
        step(False)

    @pl.when(ki == qi)
    def _diag():
        step(True)
        o_ref[0] = (acc_scr[...] / l_scr[...]).astype(o_ref.dtype)


def _moba_attention(h3, kmean):
    bsz, seq, _ = h3.shape
    t = ATTN_TILE
    assert seq % t == 0 and t % MOBA_BLOCK == 0
    n = seq // t
    nb = seq // MOBA_BLOCK
    qi_tab, ki_tab = _causal_pairs(n)
    q0 = (2 * DIFF_HEADS * DIFF_V_DIM + DIFF_HEADS * DIFF_V_DIM) // LANES
    k0 = q0 + MOBA_HEADS
    v0 = k0 + MOBA_HEADS
    grid_spec = pltpu.PrefetchScalarGridSpec(
        num_scalar_prefetch=2,
        grid=(bsz, MOBA_HEADS, int(qi_tab.shape[0])),
        in_specs=[
            pl.BlockSpec((1, t, LANES), lambda b, h, p, qt, kt: (b, qt[p], q0 + h)),
            pl.BlockSpec((1, t, LANES), lambda b, h, p, qt, kt: (b, kt[p], k0 + h)),
            pl.BlockSpec((1, t, LANES), lambda b, h, p, qt, kt: (b, kt[p], v0 + h)),
            pl.BlockSpec((1, nb, LANES), lambda b, h, p, qt, kt: (b, 0, h)),
        ],
        out_specs=pl.BlockSpec((1, t, LANES), lambda b, h, p, qt, kt: (b, qt[p], h)),
        scratch_shapes=[pltpu.VMEM((t, nb), F32),
                        pltpu.VMEM((t, 1), F32),
                        pltpu.VMEM((t, 1), F32),
                        pltpu.VMEM((t, MOBA_HEAD_DIM), F32)],
    )
    return pl.pallas_call(
        functools.partial(_moba_kernel, scale=MOBA_HEAD_DIM ** -0.5, topk=min(MOBA_TOPK, nb)),
        grid_spec=grid_spec,
        out_shape=jax.ShapeDtypeStruct((bsz, seq, MOBA_HEADS * MOBA_HEAD_DIM), BF16),
        compiler_params=_params("arbitrary", "arbitrary", "arbitrary"),
        name="moba_attn",
    )(qi_tab, ki_tab, h3, h3, h3, kmean)


def _hy_outproj_kernel(a_ref, m_ref, wa_ref, wm_ref, x_ref, g_ref, b_ref, o_ref, *, alpha):
    y = alpha * x_ref[...]
    y = y + jnp.dot(a_ref[...], wa_ref[...], preferred_element_type=F32)
    y = y + jnp.dot(m_ref[...], wm_ref[...], preferred_element_type=F32)
    o_ref[...] = _layer_norm_rows(y, g_ref[...], b_ref[...])


def _hy_outproj(a2, m2, w_out, x2, g, b, alpha):
    tokens, d = x2.shape
    wa = a2.shape[1]
    wm = m2.shape[1]
    row = lambda width: pl.BlockSpec((ROW_TILE, width), lambda i: (i, 0))
    return pl.pallas_call(
        functools.partial(_hy_outproj_kernel, alpha=alpha),
        grid=(tokens // ROW_TILE,),
        in_specs=[row(wa), row(wm), _const_spec((wa, d)), _const_spec((wm, d)), row(d),
                  _const_spec((1, d)), _const_spec((1, d))],
        out_specs=row(d),
        out_shape=jax.ShapeDtypeStruct((tokens, d), F32),
        compiler_params=_params("arbitrary"),
        name="hy_outproj_ln",
    )(a2, m2, w_out[:wa], w_out[wa:], x2, g.reshape(1, d), b.reshape(1, d))


def _mlp_kernel(x_ref, w1_ref, w2_ref, g_ref, b_ref, o_ref, *, alpha):
    x = x_ref[...]
    xb = x.astype(BF16)
    y = alpha * x
    for c in range(w1_ref.shape[1] // COL_CHUNK):
        cols = slice(c * COL_CHUNK, (c + 1) * COL_CHUNK)
        hid = jnp.maximum(jnp.dot(xb, w1_ref[:, cols], preferred_element_type=F32), 0.0)
        y = y + jnp.dot((hid * hid).astype(BF16), w2_ref[cols, :], preferred_element_type=F32)
    o_ref[...] = _layer_norm_rows(y, g_ref[...], b_ref[...])


def _mlp(x2, w1, w2, g, b, alpha):
    tokens, d = x2.shape
    dff = w1.shape[1]
    row = pl.BlockSpec((ROW_TILE, d), lambda i: (i, 0))
    return pl.pallas_call(
        functools.partial(_mlp_kernel, alpha=alpha),
        grid=(tokens // ROW_TILE,),
        in_specs=[row, _const_spec((d, dff)), _const_spec((dff, d)),
                  _const_spec((1, d)), _const_spec((1, d))],
        out_specs=row,
        out_shape=jax.ShapeDtypeStruct((tokens, d), F32),
        compiler_params=_params("arbitrary"),
        name="mlp_ln",
    )(x2, w1, w2, g.reshape(1, d), b.reshape(1, d))


def _gla_inproj_kernel(x_ref, w_ref, wd_ref, wu_ref, bg_ref, h_ref, g_ref):
    xb = x_ref[...].astype(BF16)
    for c in range(w_ref.shape[1] // COL_CHUNK):
        cols = slice(c * COL_CHUNK, (c + 1) * COL_CHUNK)
        h_ref[:, cols] = jnp.dot(xb, w_ref[:, cols], preferred_element_type=F32).astype(BF16)
    low = jnp.dot(xb, wd_ref[...], preferred_element_type=F32)
    z = jnp.dot(low.astype(BF16), wu_ref[...], preferred_element_type=F32) + bg_ref[...]
    log_sig = jnp.minimum(z, 0.0) - jnp.log1p(jnp.exp(-jnp.abs(z)))
    g_ref[...] = log_sig / GLA_GATE_NORM


def _gla_inproj(x2, w_main, w_down, w_up, b_gate):
    tokens, d = x2.shape
    width = w_main.shape[1]
    kw = w_up.shape[1]
    row = lambda w_: pl.BlockSpec((ROW_TILE, w_), lambda i: (i, 0))
    return pl.pallas_call(
        _gla_inproj_kernel,
        grid=(tokens // ROW_TILE,),
        in_specs=[row(d), _const_spec((d, width)), _const_spec((d, LANES)),
                  _const_spec((LANES, kw)), _const_spec((1, kw))],
        out_specs=[row(width), row(kw)],
        out_shape=[jax.ShapeDtypeStruct((tokens, width), BF16),
                   jax.ShapeDtypeStruct((tokens, kw), F32)],
        compiler_params=_params("arbitrary"),
        name="gla_inproj",
    )(x2, w_main, w_down, w_up, b_gate.reshape(1, kw))


def _gla_kernel(q_ref, k_ref, v_ref, g_ref, ng_ref, o_ref, st_ref, *, scale):
    L = GLA_CHUNK
    C = GLA_SUB
    n_sub = L // C
    n_chunks = q_ref.shape[1] // L
    hi = lax.Precision.HIGHEST

    @pl.when(pl.program_id(2) == 0)
    def _reset():
        st_ref[...] = jnp.zeros(st_ref.shape, F32)

    r_i = lax.broadcasted_iota(jnp.int32, (L, L), 0)
    c_i = lax.broadcasted_iota(jnp.int32, (L, L), 1)
    tri = (c_i <= r_i).astype(F32)
    ones = jnp.ones((L, GLA_DK), F32)
    sub_row = lax.broadcasted_iota(jnp.int32, (L, 1), 0) % C

    def chunk(ci, carry):
        rows = pl.ds(pl.multiple_of(ci * L, L), L)
        g = g_ref[0, rows, :]
        q = q_ref[0, rows, :].astype(F32) * scale
        k = k_ref[0, rows, :].astype(F32)
        v = v_ref[0, rows, :]
        vf = v.astype(F32)
        b = jnp.dot(tri, g, preferred_element_type=F32, precision=hi)
        state = st_ref[...]

        o = jnp.dot((q * jnp.exp(b)).astype(BF16), state.astype(BF16), preferred_element_type=F32)

        parts = [jnp.zeros((C, GLA_DV), F32)]
        for i in range(1, n_sub):
            ref_row = b[i * C - 1:i * C, :]
            qs = q[i * C:(i + 1) * C, :] * jnp.exp(b[i * C:(i + 1) * C, :] - ref_row)
            ks = k[0:i * C, :] * jnp.exp(ref_row - b[0:i * C, :])
            att = _nt_dot(qs.astype(BF16), ks.astype(BF16))
            parts.append(jnp.dot(att.astype(BF16), v[0:i * C, :], preferred_element_type=F32))
        o = o + jnp.concatenate(parts, axis=0)

        for d in range(C):
            if d == 0:
                kr, br, vr = k, b, vf
            else:
                kr = pltpu.roll(k, d, axis=0)
                br = pltpu.roll(b, d, axis=0)
                vr = pltpu.roll(vf, d, axis=0)
            w = jnp.sum(q * kr * jnp.exp(jnp.minimum(b - br, 0.0)), axis=-1, keepdims=True)
            o = o + jnp.where(sub_row >= d, w, 0.0) * vr

        y = o * lax.rsqrt(jnp.mean(o * o, axis=-1, keepdims=True) + RMS_EPS)
        o_ref[0, rows, :] = (y * ng_ref[...]).astype(o_ref.dtype)

        b_last = b[L - 1:L, :]
        kd = (k * jnp.exp(b_last - b)).astype(BF16)
        decay_log = _tn_dot(g, ones, precision=hi)
        decay = jnp.exp(decay_log)
        decay = jnp.concatenate([decay] * (GLA_DV // GLA_DK), axis=1)
        st_ref[...] = decay * state + _tn_dot(kd, v)
        return carry

    lax.fori_loop(0, n_chunks, chunk, 0)


def _gla_recurrence(h3, g3, norm_g):
    bsz, seq, _ = h3.shape
    t = GLA_STEP
    assert seq % t == 0
    kq = GLA_HEADS
    vb = (2 * GLA_HEADS * GLA_DK) // GLA_DV
    return pl.pallas_call(
        functools.partial(_gla_kernel, scale=GLA_DK ** -0.5),
        grid=(bsz, GLA_HEADS, seq // t),
        in_specs=[
            pl.BlockSpec((1, t, GLA_DK), lambda b, h, s: (b, s, h)),
            pl.BlockSpec((1, t, GLA_DK), lambda b, h, s: (b, s, kq + h)),
            pl.BlockSpec((1, t, GLA_DV), lambda b, h, s: (b, s, vb + h)),
            pl.BlockSpec((1, t, GLA_DK), lambda b, h, s: (b, s, h)),
            pl.BlockSpec((1, GLA_DV), lambda b, h, s: (0, 0)),
        ],
        out_specs=pl.BlockSpec((1, t, GLA_DV), lambda b, h, s: (b, s, h)),
        out_shape=jax.ShapeDtypeStruct((bsz, seq, GLA_HEADS * GLA_DV), BF16),
        scratch_shapes=[pltpu.VMEM((GLA_DK, GLA_DV), F32)],
        compiler_params=_params("arbitrary", "arbitrary", "arbitrary"),
        name="gla_recurrence",
    )(h3, h3, h3, g3, norm_g.reshape(1, GLA_DV))


def _gla_outproj_kernel(o_in_ref, r_ref, w_ref, x_ref, g_ref, b_ref, o_ref, *, alpha):
    r = r_ref[...].astype(F32)
    gated = o_in_ref[...].astype(F32) * (r * jax.nn.sigmoid(r))
    y = alpha * x_ref[...] + jnp.dot(gated.astype(BF16), w_ref[...], preferred_element_type=F32)
    o_ref[...] = _layer_norm_rows(y, g_ref[...], b_ref[...])


def _gla_outproj(o2, h2, w_out, x2, g, b, alpha):
    tokens, d = x2.shape
    vw = o2.shape[1]
    r_block = (h2.shape[1] - vw) // vw
    assert r_block * vw + vw == h2.shape[1]
    row = lambda width: pl.BlockSpec((ROW_TILE, width), lambda i: (i, 0))
    return pl.pallas_call(
        functools.partial(_gla_outproj_kernel, alpha=alpha),
        grid=(tokens // ROW_TILE,),
        in_specs=[row(vw), pl.BlockSpec((ROW_TILE, vw), lambda i: (i, r_block)),
                  _const_spec((vw, d)), row(d), _const_spec((1, d)), _const_spec((1, d))],
        out_specs=row(d),
        out_shape=jax.ShapeDtypeStruct((tokens, d), F32),
        compiler_params=_params("arbitrary"),
        name="gla_outproj_ln",
    )(o2, h2, w_out, x2, g.reshape(1, d), b.reshape(1, d))


def kernel(x, hy_w_in, diff_lambda, diff_subln, hy_w_out, gla_w_in, gla_w_gate_up, gla_b_gate,
           gla_norm, gla_w_out, ln_mix_g, ln_mix_b, ffn_w1, ffn_w2, ln_ffn_g, ln_ffn_b):
    bsz, seq, d = x.shape
    depth = ln_mix_g.shape[0]
    alpha = (2 * depth) ** 0.25
    tokens = bsz * seq
    assert tokens % ROW_TILE == 0

    tabs = _rope_tables(seq, DIFF_QK_DIM) + _rope_tables(seq, MOBA_HEAD_DIM)
    gla_main = 2 * GLA_HEADS * GLA_DK + 2 * GLA_HEADS * GLA_DV

    x2 = x.reshape(tokens, d)
    for layer in range(depth):
        if layer % 2 == 0:
            e = layer // 2
            lambda_init = 0.8 - 0.6 * math.exp(-0.3 * layer)
            h2, kmean = _hy_inproj(x2, hy_w_in[e].astype(BF16), tabs, seq)
            h3 = h2.reshape(bsz, seq, h2.shape[1])
            kmean = kmean.reshape(bsz, seq // MOBA_BLOCK, kmean.shape[-1])
            a = _diff_attention(h3, diff_lambda[e], diff_subln[e], lambda_init)
            m = _moba_attention(h3, kmean)
            x2 = _hy_outproj(a.reshape(tokens, -1), m.reshape(tokens, -1), hy_w_out[e].astype(BF16),
                             x2, ln_mix_g[layer], ln_mix_b[layer], alpha)
        else:
            o = layer // 2
            w_in = gla_w_in[o]
            w_down = jnp.pad(w_in[:, gla_main:], ((0, 0), (0, LANES - GLA_GATE_RANK))).astype(BF16)
            w_up = jnp.pad(gla_w_gate_up[o], ((0, LANES - GLA_GATE_RANK), (0, 0))).astype(BF16)
            h2, g2 = _gla_inproj(x2, w_in[:, :gla_main].astype(BF16), w_down, w_up, gla_b_gate[o])
            h3 = h2.reshape(bsz, seq, h2.shape[1])
            g3 = g2.reshape(bsz, seq, g2.shape[1])
            og = _gla_recurrence(h3, g3, gla_norm[o])
            x2 = _gla_outproj(og.reshape(tokens, -1), h2, gla_w_out[o].astype(BF16), x2,
                              ln_mix_g[layer], ln_mix_b[layer], alpha)
        x2 = _mlp(x2, ffn_w1[layer].astype(BF16), ffn_w2[layer].astype(BF16),
                  ln_ffn_g[layer], ln_ffn_b[layer], alpha)
    return x2.reshape(bsz, seq, d)
```

```python
import functools
import math

import jax
import jax.numpy as jnp
import numpy as np
from jax import lax
from jax.experimental import pallas as pl
from jax.experimental.pallas import tpu as pltpu

F32 = jnp.float32
BF16 = jnp.bfloat16

DIFF_HEADS = 4
DIFF_QK_DIM = 64
DIFF_V_DIM = 128
MOBA_HEADS = 4
MOBA_HEAD_DIM = 128
MOBA_BLOCK = 256
MOBA_TOPK = 3
GLA_HEADS = 4
GLA_DK = 128
GLA_DV = 256
GLA_GATE_RANK = 16
GLA_GATE_NORM = 16.0
ROPE_THETA = 10000.0
LN_EPS = 1e-5
RMS_EPS = 1e-5

LANES = 128
VMEM_LIMIT_BYTES = 56 * 1024 * 1024

ROW_TILE = 512
COL_CHUNK = 512
ATTN_TILE = 512
ATTN_ROWS = 256
LOG2E = math.log2(math.e)
GLA_CHUNK = 64
GLA_SUB = 16
GLA_STEP = 256
NEG_BIG = -1e30


def _nt_dot(a, b, precision=None):
    return lax.dot_general(a, b, (((1,), (1,)), ((), ())),
                           preferred_element_type=F32, precision=precision)


def _tn_dot(a, b, precision=None):
    return lax.dot_general(a, b, (((0,), (0,)), ((), ())),
                           preferred_element_type=F32, precision=precision)


def _const_spec(shape):
    nd = len(shape)
    return pl.BlockSpec(shape, lambda *_: (0,) * nd, pipeline_mode=pl.Buffered(1))


def _params(*sem):
    return pltpu.CompilerParams(dimension_semantics=sem, vmem_limit_bytes=VMEM_LIMIT_BYTES)


def _layer_norm_rows(y, g, b):
    mu = jnp.mean(y, axis=-1, keepdims=True)
    d = y - mu
    var = jnp.mean(d * d, axis=-1, keepdims=True)
    return d * lax.rsqrt(var + LN_EPS) * g + b


def _rope_tables(seq, group):
    half = group // 2
    inv = 1.0 / (ROPE_THETA ** (jnp.arange(0, group, 2, dtype=F32) / group))
    ang = jnp.arange(seq, dtype=F32)[:, None] * inv[None, :]
    cos, sin = jnp.cos(ang), jnp.sin(ang)
    cos_g = jnp.concatenate([cos, cos], axis=-1)
    sin_g = jnp.concatenate([-sin, sin], axis=-1)
    reps = LANES // group
    return jnp.tile(cos_g, (1, reps)), jnp.tile(sin_g, (1, reps))


def _rope(acc, cos, sin, group):
    width = acc.shape[1]
    half = group // 2
    lane = lax.broadcasted_iota(jnp.int32, acc.shape, 1)
    upper = pltpu.roll(acc, width - half, axis=1)
    lower = pltpu.roll(acc, half, axis=1)
    partner = jnp.where((lane % group) < half, upper, lower)
    reps = width // LANES
    c = jnp.concatenate([cos] * reps, axis=1)
    s = jnp.concatenate([sin] * reps, axis=1)
    return acc * c + partner * s


def _hy_inproj_kernel(x_ref, w_ref, cd_ref, sd_ref, cm_ref, sm_ref, h_ref, km_ref):
    xb = x_ref[...].astype(BF16)
    rows = xb.shape[0]
    n_chunks = w_ref.shape[1] // COL_CHUNK
    for c in range(n_chunks):
        cols = slice(c * COL_CHUNK, (c + 1) * COL_CHUNK)
        acc = jnp.dot(xb, w_ref[:, cols], preferred_element_type=F32)
        if c == 0:
            acc = _rope(acc, cd_ref[...], sd_ref[...], DIFF_QK_DIM) * (DIFF_QK_DIM ** -0.5)
        elif c == 1:
            acc = _rope(acc, cd_ref[...], sd_ref[...], DIFF_QK_DIM)
        elif c in (3, 4):
            acc = _rope(acc, cm_ref[...], sm_ref[...], MOBA_HEAD_DIM)
        if c == 4:
            blocks = rows // MOBA_BLOCK
            km_ref[0] = jnp.mean(acc.reshape(blocks, MOBA_BLOCK, COL_CHUNK), axis=1)
        h_ref[:, cols] = acc.astype(BF16)


def _hy_inproj(x2, w, tabs, seq):
    tokens, d = x2.shape
    width = w.shape[1]
    assert width == 6 * COL_CHUNK and seq % ROW_TILE == 0 and ROW_TILE % MOBA_BLOCK == 0
    steps = tokens // ROW_TILE
    per_seq = seq // ROW_TILE
    blocks = ROW_TILE // MOBA_BLOCK
    tab_spec = pl.BlockSpec((ROW_TILE, LANES), lambda i: (i % per_seq, 0))
    return pl.pallas_call(
        _hy_inproj_kernel,
        grid=(steps,),
        in_specs=[pl.BlockSpec((ROW_TILE, d), lambda i: (i, 0)),
                  _const_spec((d, width)),
                  tab_spec, tab_spec, tab_spec, tab_spec],
        out_specs=[pl.BlockSpec((ROW_TILE, width), lambda i: (i, 0)),
                   pl.BlockSpec((1, blocks, COL_CHUNK), lambda i: (i, 0, 0))],
        out_shape=[jax.ShapeDtypeStruct((tokens, width), BF16),
                   jax.ShapeDtypeStruct((steps, blocks, COL_CHUNK), F32)],
        compiler_params=_params("arbitrary"),
        name="hy_inproj",
    )(x2, w, *tabs)


def _causal_pairs(n):
    qi = np.array([q for q in range(n) for _ in range(q + 1)], np.int32)
    ki = np.array([k for q in range(n) for k in range(q + 1)], np.int32)
    return jnp.asarray(qi), jnp.asarray(ki)


def _with_ones(v):
    return jnp.concatenate([v, jnp.ones(v.shape, v.dtype)], axis=1)


def _flash_update(q_rows, k, v_ext, m_ref, acc_ref, rows, exp_scale, mask=None):
    s = _nt_dot(q_rows, k)
    if mask is not None:
        s = jnp.where(mask, s, NEG_BIG)
    reps = s.shape[1] // LANES
    m_prev = m_ref[rows, :]
    m_new = jnp.maximum(m_prev, jnp.max(s, axis=-1, keepdims=True))
    alpha = jnp.exp2((m_prev - m_new) * exp_scale)
    p = jnp.exp2((s - jnp.concatenate([m_new] * reps, axis=1)) * exp_scale)
    pv = jnp.dot(p.astype(BF16), v_ext, preferred_element_type=F32)
    acc_ref[rows, :] = jnp.concatenate([alpha, alpha], axis=1) * acc_ref[rows, :] + pv
    m_ref[rows, :] = m_new


def _diff_attn_kernel(qi_tab, ki_tab, q_ref, k_ref, v_ref, lam_ref, sub_ref, o_ref,
                      qs_scr, m_scr, acc_scr, *, lambda_init):
    p_id = pl.program_id(2)
    qi = qi_tab[p_id]
    ki = ki_tab[p_id]
    tq = q_ref.shape[1]
    tk = k_ref.shape[1]
    rb = ATTN_ROWS

    @pl.when(ki == 0)
    def _init():
        q = q_ref[0]
        lane = lax.broadcasted_iota(jnp.int32, q.shape, 1)
        zero = jnp.zeros_like(q)
        qs_scr[0:tq, :] = jnp.where(lane < DIFF_QK_DIM, q, zero)
        qs_scr[tq:2 * tq, :] = jnp.where(lane >= DIFF_QK_DIM, q, zero)
        m_scr[...] = jnp.full(m_scr.shape, NEG_BIG, F32)
        acc_scr[...] = jnp.zeros(acc_scr.shape, F32)

    def step(masked):
        k = k_ref[0]
        v_ext = _with_ones(v_ref[0])
        for r in range(2 * tq // rb):
            rows = slice(r * rb, (r + 1) * rb)
            mask = None
            if masked:
                row = lax.broadcasted_iota(jnp.int32, (rb, tk), 0) + (r * rb) % tq
                col = lax.broadcasted_iota(jnp.int32, (rb, tk), 1)
                mask = col <= row
            _flash_update(qs_scr[rows, :], k, v_ext, m_scr, acc_scr, rows, LOG2E, mask)

    @pl.when(ki < qi)
    def _full():
        step(False)

    @pl.when(ki == qi)
    def _diag():
        step(True)
        lam = lam_ref[...].astype(F32)
        lam_full = (jnp.exp(jnp.sum(lam[0:1] * lam[1:2], axis=-1, keepdims=True))
                    - jnp.exp(jnp.sum(lam[2:3] * lam[3:4], axis=-1, keepdims=True))
                    + lambda_init)
        o1 = acc_scr[0:tq, 0:DIFF_V_DIM] / acc_scr[0:tq, DIFF_V_DIM:]
        o2 = acc_scr[tq:2 * tq, 0:DIFF_V_DIM] / acc_scr[tq:2 * tq, DIFF_V_DIM:]
        o = o1 - lam_full * o2
        y = o * lax.rsqrt(jnp.mean(o * o, axis=-1, keepdims=True) + RMS_EPS)
        o_ref[0] = (y * sub_ref[...] * (1.0 - lambda_init)).astype(o_ref.dtype)


def _diff_attention(h3, lam, subln, lambda_init):
    bsz, seq, _ = h3.shape
    t = ATTN_TILE
    assert seq % t == 0
    n = seq // t
    qi_tab, ki_tab = _causal_pairs(n)
    hb = DIFF_HEADS
    grid_spec = pltpu.PrefetchScalarGridSpec(
        num_scalar_prefetch=2,
        grid=(bsz, DIFF_HEADS, int(qi_tab.shape[0])),
        in_specs=[
            pl.BlockSpec((1, t, LANES), lambda b, h, p, qt, kt: (b, qt[p], h)),
            pl.BlockSpec((1, t, LANES), lambda b, h, p, qt, kt: (b, kt[p], hb + h)),
            pl.BlockSpec((1, t, LANES), lambda b, h, p, qt, kt: (b, kt[p], 2 * hb + h)),
            pl.BlockSpec((4, DIFF_QK_DIM), lambda b, h, p, qt, kt: (0, 0)),
            pl.BlockSpec((1, DIFF_V_DIM), lambda b, h, p, qt, kt: (0, 0)),
        ],
        out_specs=pl.BlockSpec((1, t, LANES), lambda b, h, p, qt, kt: (b, qt[p], h)),
        scratch_shapes=[pltpu.VMEM((2 * t, LANES), BF16),
                        pltpu.VMEM((2 * t, LANES), F32),
                        pltpu.VMEM((2 * t, 2 * DIFF_V_DIM), F32)],
    )
    return pl.pallas_call(
        functools.partial(_diff_attn_kernel, lambda_init=lambda_init),
        grid_spec=grid_spec,
        out_shape=jax.ShapeDtypeStruct((bsz, seq, DIFF_HEADS * DIFF_V_DIM), BF16),
        compiler_params=_params("arbitrary", "arbitrary", "arbitrary"),
        name="diff_attn",
    )(qi_tab, ki_tab, h3, h3, h3, lam, subln.reshape(1, DIFF_V_DIM))


def _moba_kernel(qi_tab, ki_tab, q_ref, k_ref, v_ref, km_ref, oh_ref, o_ref,
                 qa_scr, m_scr, acc_scr, *, exp_scale, topk):
    p_id = pl.program_id(2)
    qi = qi_tab[p_id]
    ki = ki_tab[p_id]
    tq = q_ref.shape[1]
    tk = k_ref.shape[1]
    d = q_ref.shape[2]
    rb = ATTN_ROWS

    @pl.when(ki == 0)
    def _init():
        q = q_ref[0]
        gate = _nt_dot(q.astype(F32), km_ref[0], precision=lax.Precision.HIGHEST)
        blk = lax.broadcasted_iota(jnp.int32, gate.shape, 1)
        row = lax.broadcasted_iota(jnp.int32, (tq, 1), 0)
        own = (qi * tq + row) // MOBA_BLOCK
        g = jnp.where(blk < own, gate, -jnp.inf)
        sel = blk == own
        for _ in range(topk):
            mx = jnp.max(g, axis=-1, keepdims=True)
            idx = jnp.min(jnp.where(g == mx, blk, LANES), axis=-1, keepdims=True)
            sel = sel | ((blk == idx) & (mx > -jnp.inf))
            g = jnp.where(blk == idx, -jnp.inf, g)
        qa_scr[:, 0:d] = q
        qa_scr[:, d:2 * d] = jnp.where(sel, 0.0, NEG_BIG).astype(BF16)
        m_scr[...] = jnp.full(m_scr.shape, NEG_BIG, F32)
        acc_scr[...] = jnp.zeros(acc_scr.shape, F32)

    def step(diagonal):
        k_aug = jnp.concatenate([k_ref[0], oh_ref[...]], axis=1)
        v_ext = _with_ones(v_ref[0])
        for r in range(tq // rb):
            rows = slice(r * rb, (r + 1) * rb)
            mask = None
            if diagonal:
                row = lax.broadcasted_iota(jnp.int32, (rb, tk), 0) + r * rb
                col = lax.broadcasted_iota(jnp.int32, (rb, tk), 1)
                mask = col <= row
            _flash_update(qa_scr[rows, :], k_aug, v_ext, m_scr, acc_scr, rows, exp_scale, mask)

    @pl.when(ki < qi)
    def _past():
        step(False)

    @pl.when(ki == qi)
    def _diag():
        step(True)
        o_ref[0] = (acc_scr[:, 0:d] / acc_scr[:, d:2 * d]).astype(o_ref.dtype)


def _moba_attention(h3, kmean):
    bsz, seq, _ = h3.shape
    t = ATTN_TILE
    nb = seq // MOBA_BLOCK
    assert seq % t == 0 and t % MOBA_BLOCK == 0 and nb <= LANES and MOBA_HEAD_DIM == LANES
    n = seq // t
    qi_tab, ki_tab = _causal_pairs(n)
    q0 = (2 * DIFF_HEADS * DIFF_V_DIM + DIFF_HEADS * DIFF_V_DIM) // LANES
    k0 = q0 + MOBA_HEADS
    v0 = k0 + MOBA_HEADS
    kmean = jnp.pad(kmean, ((0, 0), (0, LANES - nb), (0, 0)))
    block_of_key = jnp.arange(seq, dtype=jnp.int32)[:, None] // MOBA_BLOCK
    onehot = (block_of_key == jnp.arange(LANES, dtype=jnp.int32)[None, :]).astype(BF16)
    grid_spec = pltpu.PrefetchScalarGridSpec(
        num_scalar_prefetch=2,
        grid=(bsz, MOBA_HEADS, int(qi_tab.shape[0])),
        in_specs=[
            pl.BlockSpec((1, t, LANES), lambda b, h, p, qt, kt: (b, qt[p], q0 + h)),
            pl.BlockSpec((1, t, LANES), lambda b, h, p, qt, kt: (b, kt[p], k0 + h)),
            pl.BlockSpec((1, t, LANES), lambda b, h, p, qt, kt: (b, kt[p], v0 + h)),
            pl.BlockSpec((1, LANES, LANES), lambda b, h, p, qt, kt: (b, 0, h)),
            pl.BlockSpec((t, LANES), lambda b, h, p, qt, kt: (kt[p], 0)),
        ],
        out_specs=pl.BlockSpec((1, t, LANES), lambda b, h, p, qt, kt: (b, qt[p], h)),
        scratch_shapes=[pltpu.VMEM((t, 2 * MOBA_HEAD_DIM), BF16),
                        pltpu.VMEM((t, LANES), F32),
                        pltpu.VMEM((t, 2 * MOBA_HEAD_DIM), F32)],
    )
    return pl.pallas_call(
        functools.partial(_moba_kernel, exp_scale=MOBA_HEAD_DIM ** -0.5 * LOG2E,
                          topk=min(MOBA_TOPK, nb)),
        grid_spec=grid_spec,
        out_shape=jax.ShapeDtypeStruct((bsz, seq, MOBA_HEADS * MOBA_HEAD_DIM), BF16),
        compiler_params=_params("arbitrary", "arbitrary", "arbitrary"),
        name="moba_attn",
    )(qi_tab, ki_tab, h3, h3, h3, kmean, onehot)


def _hy_outproj_kernel(a_ref, m_ref, wa_ref, wm_ref, x_ref, g_ref, b_ref, o_ref, *, alpha):
    y = alpha * x_ref[...]
    y = y + jnp.dot(a_ref[...], wa_ref[...], preferred_element_type=F32)
    y = y + jnp.dot(m_ref[...], wm_ref[...], preferred_element_type=F32)
    o_ref[...] = _layer_norm_rows(y, g_ref[...], b_ref[...])


def _hy_outproj(a2, m2, w_out, x2, g, b, alpha):
    tokens, d = x2.shape
    wa = a2.shape[1]
    wm = m2.shape[1]
    row = lambda width: pl.BlockSpec((ROW_TILE, width), lambda i: (i, 0))
    return pl.pallas_call(
        functools.partial(_hy_outproj_kernel, alpha=alpha),
        grid=(tokens // ROW_TILE,),
        in_specs=[row(wa), row(wm), _const_spec((wa, d)), _const_spec((wm, d)), row(d),
                  _const_spec((1, d)), _const_spec((1, d))],
        out_specs=row(d),
        out_shape=jax.ShapeDtypeStruct((tokens, d), F32),
        compiler_params=_params("arbitrary"),
        name="hy_outproj_ln",
    )(a2, m2, w_out[:wa], w_out[wa:], x2, g.reshape(1, d), b.reshape(1, d))


def _mlp_kernel(x_ref, w1_ref, w2_ref, g_ref, b_ref, o_ref, *, alpha):
    x = x_ref[...]
    xb = x.astype(BF16)
    y = alpha * x
    for c in range(w1_ref.shape[1] // COL_CHUNK):
        cols = slice(c * COL_CHUNK, (c + 1) * COL_CHUNK)
        hid = jnp.maximum(jnp.dot(xb, w1_ref[:, cols], preferred_element_type=F32), 0.0)
        y = y + jnp.dot((hid * hid).astype(BF16), w2_ref[cols, :], preferred_element_type=F32)
    o_ref[...] = _layer_norm_rows(y, g_ref[...], b_ref[...])


def _mlp(x2, w1, w2, g, b, alpha):
    tokens, d = x2.shape
    dff = w1.shape[1]
    row = pl.BlockSpec((ROW_TILE, d), lambda i: (i, 0))
    return pl.pallas_call(
        functools.partial(_mlp_kernel, alpha=alpha),
        grid=(tokens // ROW_TILE,),
        in_specs=[row, _const_spec((d, dff)), _const_spec((dff, d)),
                  _const_spec((1, d)), _const_spec((1, d))],
        out_specs=row,
        out_shape=jax.ShapeDtypeStruct((tokens, d), F32),
        compiler_params=_params("arbitrary"),
        name="mlp_ln",
    )(x2, w1, w2, g.reshape(1, d), b.reshape(1, d))


def _gla_inproj_kernel(x_ref, w_ref, wd_ref, wu_ref, bg_ref, h_ref, g_ref):
    xb = x_ref[...].astype(BF16)
    for c in range(w_ref.shape[1] // COL_CHUNK):
        cols = slice(c * COL_CHUNK, (c + 1) * COL_CHUNK)
        h_ref[:, cols] = jnp.dot(xb, w_ref[:, cols], preferred_element_type=F32).astype(BF16)
    low = jnp.dot(xb, wd_ref[...], preferred_element_type=F32)
    z = jnp.dot(low.astype(BF16), wu_ref[...], preferred_element_type=F32) + bg_ref[...]
    log_sig = jnp.minimum(z, 0.0) - jnp.log1p(jnp.exp(-jnp.abs(z)))
    g_ref[...] = log_sig / GLA_GATE_NORM


def _gla_inproj(x2, w_main, w_down, w_up, b_gate):
    tokens, d = x2.shape
    width = w_main.shape[1]
    kw = w_up.shape[1]
    row = lambda w_: pl.BlockSpec((ROW_TILE, w_), lambda i: (i, 0))
    return pl.pallas_call(
        _gla_inproj_kernel,
        grid=(tokens // ROW_TILE,),
        in_specs=[row(d), _const_spec((d, width)), _const_spec((d, LANES)),
                  _const_spec((LANES, kw)), _const_spec((1, kw))],
        out_specs=[row(width), row(kw)],
        out_shape=[jax.ShapeDtypeStruct((tokens, width), BF16),
                   jax.ShapeDtypeStruct((tokens, kw), F32)],
        compiler_params=_params("arbitrary"),
        name="gla_inproj",
    )(x2, w_main, w_down, w_up, b_gate.reshape(1, kw))


def _gla_kernel(q_ref, k_ref, v_ref, g_ref, ng_ref, o_ref, st_ref, *, scale):
    L = GLA_CHUNK
    C = GLA_SUB
    n_sub = L // C
    n_chunks = q_ref.shape[1] // L
    hi = lax.Precision.HIGHEST

    @pl.when(pl.program_id(2) == 0)
    def _reset():
        st_ref[...] = jnp.zeros(st_ref.shape, F32)

    r_i = lax.broadcasted_iota(jnp.int32, (L, L), 0)
    c_i = lax.broadcasted_iota(jnp.int32, (L, L), 1)
    tri = (c_i <= r_i).astype(F32)
    ones = jnp.ones((L, GLA_DK), F32)
    sub_row = lax.broadcasted_iota(jnp.int32, (L, 1), 0) % C

    def chunk(ci, carry):
        rows = pl.ds(pl.multiple_of(ci * L, L), L)
        g = g_ref[0, rows, :]
        q = q_ref[0, rows, :].astype(F32) * scale
        k = k_ref[0, rows, :].astype(F32)
        v = v_ref[0, rows, :]
        vf = v.astype(F32)
        b = jnp.dot(tri, g, preferred_element_type=F32, precision=hi)
        state = st_ref[...]

        o = jnp.dot((q * jnp.exp(b)).astype(BF16), state.astype(BF16), preferred_element_type=F32)

        parts = [jnp.zeros((C, GLA_DV), F32)]
        for i in range(1, n_sub):
            ref_row = b[i * C - 1:i * C, :]
            qs = q[i * C:(i + 1) * C, :] * jnp.exp(b[i * C:(i + 1) * C, :] - ref_row)
            ks = k[0:i * C, :] * jnp.exp(ref_row - b[0:i * C, :])
            att = _nt_dot(qs.astype(BF16), ks.astype(BF16))
            parts.append(jnp.dot(att.astype(BF16), v[0:i * C, :], preferred_element_type=F32))
        o = o + jnp.concatenate(parts, axis=0)

        for d in range(C):
            if d == 0:
                kr, br, vr = k, b, vf
            else:
                kr = pltpu.roll(k, d, axis=0)
                br = pltpu.roll(b, d, axis=0)
                vr = pltpu.roll(vf, d, axis=0)
            w = jnp.sum(q * kr * jnp.exp(jnp.minimum(b - br, 0.0)), axis=-1, keepdims=True)
            o = o + jnp.where(sub_row >= d, w, 0.0) * vr

        y = o * lax.rsqrt(jnp.mean(o * o, axis=-1, keepdims=True) + RMS_EPS)
        o_ref[0, rows, :] = (y * ng_ref[...]).astype(o_ref.dtype)

        b_last = b[L - 1:L, :]
        kd = (k * jnp.exp(b_last - b)).astype(BF16)
        decay_log = _tn_dot(g, ones, precision=hi)
        decay = jnp.exp(decay_log)
        decay = jnp.concatenate([decay] * (GLA_DV // GLA_DK), axis=1)
        st_ref[...] = decay * state + _tn_dot(kd, v)
        return carry

    lax.fori_loop(0, n_chunks, chunk, 0)


def _gla_recurrence(h3, g3, norm_g):
    bsz, seq, _ = h3.shape
    t = GLA_STEP
    assert seq % t == 0
    kq = GLA_HEADS
    vb = (2 * GLA_HEADS * GLA_DK) // GLA_DV
    return pl.pallas_call(
        functools.partial(_gla_kernel, scale=GLA_DK ** -0.5),
        grid=(bsz, GLA_HEADS, seq // t),
        in_specs=[
            pl.BlockSpec((1, t, GLA_DK), lambda b, h, s: (b, s, h)),
            pl.BlockSpec((1, t, GLA_DK), lambda b, h, s: (b, s, kq + h)),
            pl.BlockSpec((1, t, GLA_DV), lambda b, h, s: (b, s, vb + h)),
            pl.BlockSpec((1, t, GLA_DK), lambda b, h, s: (b, s, h)),
            pl.BlockSpec((1, GLA_DV), lambda b, h, s: (0, 0)),
        ],
        out_specs=pl.BlockSpec((1, t, GLA_DV), lambda b, h, s: (b, s, h)),
        out_shape=jax.ShapeDtypeStruct((bsz, seq, GLA_HEADS * GLA_DV), BF16),
        scratch_shapes=[pltpu.VMEM((GLA_DK, GLA_DV), F32)],
        compiler_params=_params("arbitrary", "arbitrary", "arbitrary"),
        name="gla_recurrence",
    )(h3, h3, h3, g3, norm_g.reshape(1, GLA_DV))


def _gla_outproj_kernel(o_in_ref, r_ref, w_ref, x_ref, g_ref, b_ref, o_ref, *, alpha):
    r = r_ref[...].astype(F32)
    gated = o_in_ref[...].astype(F32) * (r * jax.nn.sigmoid(r))
    y = alpha * x_ref[...] + jnp.dot(gated.astype(BF16), w_ref[...], preferred_element_type=F32)
    o_ref[...] = _layer_norm_rows(y, g_ref[...], b_ref[...])


def _gla_outproj(o2, h2, w_out, x2, g, b, alpha):
    tokens, d = x2.shape
    vw = o2.shape[1]
    r_block = (h2.shape[1] - vw) // vw
    assert r_block * vw + vw == h2.shape[1]
    row = lambda width: pl.BlockSpec((ROW_TILE, width), lambda i: (i, 0))
    return pl.pallas_call(
        functools.partial(_gla_outproj_kernel, alpha=alpha),
        grid=(tokens // ROW_TILE,),
        in_specs=[row(vw), pl.BlockSpec((ROW_TILE, vw), lambda i: (i, r_block)),
                  _const_spec((vw, d)), row(d), _const_spec((1, d)), _const_spec((1, d))],
        out_specs=row(d),
        out_shape=jax.ShapeDtypeStruct((tokens, d), F32),
        compiler_params=_params("arbitrary"),
        name="gla_outproj_ln",
    )(o2, h2, w_out, x2, g.reshape(1, d), b.reshape(1, d))


def kernel(x, hy_w_in, diff_lambda, diff_subln, hy_w_out, gla_w_in, gla_w_gate_up, gla_b_gate,
           gla_norm, gla_w_out, ln_mix_g, ln_mix_b, ffn_w1, ffn_w2, ln_ffn_g, ln_ffn_b):
    bsz, seq, d = x.shape
    depth = ln_mix_g.shape[0]
    alpha = (2 * depth) ** 0.25
    tokens = bsz * seq
    assert tokens % ROW_TILE == 0

    tabs = _rope_tables(seq, DIFF_QK_DIM) + _rope_tables(seq, MOBA_HEAD_DIM)
    gla_main = 2 * GLA_HEADS * GLA_DK + 2 * GLA_HEADS * GLA_DV

    x2 = x.reshape(tokens, d)
    for layer in range(depth):
        if layer % 2 == 0:
            e = layer // 2
            lambda_init = 0.8 - 0.6 * math.exp(-0.3 * layer)
            h2, kmean = _hy_inproj(x2, hy_w_in[e].astype(BF16), tabs, seq)
            h3 = h2.reshape(bsz, seq, h2.shape[1])
            kmean = kmean.reshape(bsz, seq // MOBA_BLOCK, kmean.shape[-1])
            a = _diff_attention(h3, diff_lambda[e], diff_subln[e], lambda_init)
            m = _moba_attention(h3, kmean)
            x2 = _hy_outproj(a.reshape(tokens, -1), m.reshape(tokens, -1), hy_w_out[e].astype(BF16),
                             x2, ln_mix_g[layer], ln_mix_b[layer], alpha)
        else:
            o = layer // 2
            w_in = gla_w_in[o]
            w_down = jnp.pad(w_in[:, gla_main:], ((0, 0), (0, LANES - GLA_GATE_RANK))).astype(BF16)
            w_up = jnp.pad(gla_w_gate_up[o], ((0, LANES - GLA_GATE_RANK), (0, 0))).astype(BF16)
            h2, g2 = _gla_inproj(x2, w_in[:, :gla_main].astype(BF16), w_down, w_up, gla_b_gate[o])
            h3 = h2.reshape(bsz, seq, h2.shape[1])
            g3 = g2.reshape(bsz, seq, g2.shape[1])
            og = _gla_recurrence(h3, g3, gla_norm[o])
            x2 = _gla_outproj(og.reshape(tokens, -1), h2, gla_w_out[o].astype(BF16), x2,
                              ln_mix_g[layer], ln_mix_b[layer], alpha)
        x2 = _mlp(x2, ffn_w1[layer].astype(BF16), ffn_w2[layer].astype(BF16),
                  ln_ffn_g[layer], ln_ffn_b[layer], alpha)
    return x2.reshape(bsz, seq, d)
```

```python
import functools
import math

import jax
import jax.numpy as jnp
import numpy as np
from jax import lax
from jax.experimental import pallas as pl
from jax.experimental.pallas import tpu as pltpu

F32 = jnp.float32
BF16 = jnp.bfloat16

DIFF_HEADS = 4
DIFF_QK_DIM = 64
DIFF_V_DIM = 128
MOBA_HEADS = 4
MOBA_HEAD_DIM = 128
MOBA_BLOCK = 256
MOBA_TOPK = 3
GLA_HEADS = 4
GLA_DK = 128
GLA_DV = 256
GLA_GATE_RANK = 16
GLA_GATE_NORM = 16.0
ROPE_THETA = 10000.0
LN_EPS = 1e-5
RMS_EPS = 1e-5

LANES = 128
VMEM_LIMIT_BYTES = 56 * 1024 * 1024

ROW_TILE = 512
COL_CHUNK = 512
ATTN_TILE = 512
ATTN_ROWS = 128
ATTN_HEADS_PER_STEP = 2
LOG2E = math.log2(math.e)
GLA_CHUNK = 64
GLA_SUB = 16
GLA_STEP = 256
NEG_BIG = -1e30


def _nt_dot(a, b, precision=None):
    return lax.dot_general(a, b, (((1,), (1,)), ((), ())),
                           preferred_element_type=F32, precision=precision)


def _tn_dot(a, b, precision=None):
    return lax.dot_general(a, b, (((0,), (0,)), ((), ())),
                           preferred_element_type=F32, precision=precision)


def _const_spec(shape):
    nd = len(shape)
    return pl.BlockSpec(shape, lambda *_: (0,) * nd, pipeline_mode=pl.Buffered(1))


def _params(*sem):
    return pltpu.CompilerParams(dimension_semantics=sem, vmem_limit_bytes=VMEM_LIMIT_BYTES)


def _layer_norm_rows(y, g, b):
    mu = jnp.mean(y, axis=-1, keepdims=True)
    d = y - mu
    var = jnp.mean(d * d, axis=-1, keepdims=True)
    return d * lax.rsqrt(var + LN_EPS) * g + b


def _rope_tables(seq, group):
    half = group // 2
    inv = 1.0 / (ROPE_THETA ** (jnp.arange(0, group, 2, dtype=F32) / group))
    ang = jnp.arange(seq, dtype=F32)[:, None] * inv[None, :]
    cos, sin = jnp.cos(ang), jnp.sin(ang)
    cos_g = jnp.concatenate([cos, cos], axis=-1)
    sin_g = jnp.concatenate([-sin, sin], axis=-1)
    reps = LANES // group
    return jnp.tile(cos_g, (1, reps)), jnp.tile(sin_g, (1, reps))


def _rope(acc, cos, sin, group):
    width = acc.shape[1]
    half = group // 2
    lane = lax.broadcasted_iota(jnp.int32, acc.shape, 1)
    upper = pltpu.roll(acc, width - half, axis=1)
    lower = pltpu.roll(acc, half, axis=1)
    partner = jnp.where((lane % group) < half, upper, lower)
    reps = width // LANES
    c = jnp.concatenate([cos] * reps, axis=1)
    s = jnp.concatenate([sin] * reps, axis=1)
    return acc * c + partner * s


def _hy_inproj_kernel(x_ref, w_ref, cd_ref, sd_ref, cm_ref, sm_ref, h_ref, km_ref):
    xb = x_ref[...].astype(BF16)
    rows = xb.shape[0]
    n_chunks = w_ref.shape[1] // COL_CHUNK
    for c in range(n_chunks):
        cols = slice(c * COL_CHUNK, (c + 1) * COL_CHUNK)
        acc = jnp.dot(xb, w_ref[:, cols], preferred_element_type=F32)
        if c == 0:
            acc = _rope(acc, cd_ref[...], sd_ref[...], DIFF_QK_DIM) * (DIFF_QK_DIM ** -0.5 * LOG2E)
        elif c == 1:
            acc = _rope(acc, cd_ref[...], sd_ref[...], DIFF_QK_DIM)
        elif c == 3:
            acc = _rope(acc, cm_ref[...], sm_ref[...], MOBA_HEAD_DIM) * (MOBA_HEAD_DIM ** -0.5 * LOG2E)
        elif c == 4:
            acc = _rope(acc, cm_ref[...], sm_ref[...], MOBA_HEAD_DIM)
        if c == 4:
            blocks = rows // MOBA_BLOCK
            km_ref[0] = jnp.mean(acc.reshape(blocks, MOBA_BLOCK, COL_CHUNK), axis=1)
        h_ref[:, cols] = acc.astype(BF16)


def _hy_inproj(x2, w, tabs, seq):
    tokens, d = x2.shape
    width = w.shape[1]
    assert width == 6 * COL_CHUNK and seq % ROW_TILE == 0 and ROW_TILE % MOBA_BLOCK == 0
    steps = tokens // ROW_TILE
    per_seq = seq // ROW_TILE
    blocks = ROW_TILE // MOBA_BLOCK
    tab_spec = pl.BlockSpec((ROW_TILE, LANES), lambda i: (i % per_seq, 0))
    return pl.pallas_call(
        _hy_inproj_kernel,
        grid=(steps,),
        in_specs=[pl.BlockSpec((ROW_TILE, d), lambda i: (i, 0)),
                  _const_spec((d, width)),
                  tab_spec, tab_spec, tab_spec, tab_spec],
        out_specs=[pl.BlockSpec((ROW_TILE, width), lambda i: (i, 0)),
                   pl.BlockSpec((1, blocks, COL_CHUNK), lambda i: (i, 0, 0))],
        out_shape=[jax.ShapeDtypeStruct((tokens, width), BF16),
                   jax.ShapeDtypeStruct((steps, blocks, COL_CHUNK), F32)],
        compiler_params=_params("arbitrary"),
        name="hy_inproj",
    )(x2, w, *tabs)


def _causal_pairs(n):
    qi = np.array([q for q in range(n) for _ in range(q + 1)], np.int32)
    ki = np.array([k for q in range(n) for k in range(q + 1)], np.int32)
    return jnp.asarray(qi), jnp.asarray(ki)


def _with_ones(v):
    return jnp.concatenate([v, jnp.ones(v.shape, v.dtype)], axis=1)


def _flash_update(s, v_ext, m_ref, acc_ref, rows):
    reps = s.shape[1] // LANES
    m_prev = m_ref[rows, :]
    m_new = jnp.maximum(m_prev, jnp.max(s, axis=-1, keepdims=True))
    alpha = jnp.exp2(m_prev - m_new)
    p = jnp.exp2(s - jnp.concatenate([m_new] * reps, axis=1))
    pv = jnp.dot(p.astype(BF16), v_ext, preferred_element_type=F32)
    acc_ref[rows, :] = jnp.concatenate([alpha, alpha], axis=1) * acc_ref[rows, :] + pv
    m_ref[rows, :] = m_new


def _attend_tile(q_all, k, v, m_ref, acc_ref, row0, tq, diagonal):
    rb = ATTN_ROWS
    v_ext = _with_ones(v)
    s_all = _nt_dot(q_all, k)
    for r in range(q_all.shape[0] // rb):
        rows = slice(row0 + r * rb, row0 + (r + 1) * rb)
        s = s_all[r * rb:(r + 1) * rb]
        if diagonal:
            first = (r * rb) % tq
            keys = first + rb
            row = lax.broadcasted_iota(jnp.int32, (rb, keys), 0) + first
            col = lax.broadcasted_iota(jnp.int32, (rb, keys), 1)
            s = jnp.where(col <= row, s[:, 0:keys], NEG_BIG)
            _flash_update(s, v_ext[0:keys], m_ref, acc_ref, rows)
        else:
            _flash_update(s, v_ext, m_ref, acc_ref, rows)


def _diff_attn_kernel(qi_tab, ki_tab, q_ref, k_ref, v_ref, lam_ref, sub_ref, o_ref,
                      qs_scr, m_scr, acc_scr, *, lambda_init):
    p_id = pl.program_id(2)
    qi = qi_tab[p_id]
    ki = ki_tab[p_id]
    tq = q_ref.shape[1]
    rb = ATTN_ROWS
    heads = q_ref.shape[2] // LANES
    dv = DIFF_V_DIM

    @pl.when(ki == 0)
    def _init():
        for hh in range(heads):
            q = q_ref[0, :, hh * LANES:(hh + 1) * LANES]
            lane = lax.broadcasted_iota(jnp.int32, q.shape, 1)
            zero = jnp.zeros_like(q)
            base = hh * 2 * tq
            qs_scr[base:base + tq, :] = jnp.where(lane < DIFF_QK_DIM, q, zero)
            qs_scr[base + tq:base + 2 * tq, :] = jnp.where(lane >= DIFF_QK_DIM, q, zero)
        m_scr[...] = jnp.full(m_scr.shape, NEG_BIG, F32)
        acc_scr[...] = jnp.zeros(acc_scr.shape, F32)

    def step(diagonal):
        for hh in range(heads):
            base = hh * 2 * tq
            _attend_tile(qs_scr[base:base + 2 * tq, :], k_ref[0, :, hh * LANES:(hh + 1) * LANES],
                         v_ref[0, :, hh * dv:(hh + 1) * dv], m_scr, acc_scr, base, tq, diagonal)

    @pl.when(ki < qi)
    def _full():
        step(False)

    @pl.when(ki == qi)
    def _diag():
        step(True)
        lam = lam_ref[...].astype(F32)
        lam_full = (jnp.exp(jnp.sum(lam[0:1] * lam[1:2], axis=-1, keepdims=True))
                    - jnp.exp(jnp.sum(lam[2:3] * lam[3:4], axis=-1, keepdims=True))
                    + lambda_init)
        for hh in range(heads):
            base = hh * 2 * tq
            o1 = acc_scr[base:base + tq, 0:dv] / acc_scr[base:base + tq, dv:]
            o2 = acc_scr[base + tq:base + 2 * tq, 0:dv] / acc_scr[base + tq:base + 2 * tq, dv:]
            o = o1 - lam_full * o2
            y = o * lax.rsqrt(jnp.mean(o * o, axis=-1, keepdims=True) + RMS_EPS)
            o_ref[0, :, hh * dv:(hh + 1) * dv] = (y * sub_ref[...] * (1.0 - lambda_init)).astype(o_ref.dtype)


def _diff_attention(h3, lam, subln, lambda_init):
    bsz, seq, _ = h3.shape
    t = ATTN_TILE
    assert seq % t == 0
    n = seq // t
    qi_tab, ki_tab = _causal_pairs(n)
    hp = ATTN_HEADS_PER_STEP
    assert DIFF_HEADS % hp == 0 and 2 * DIFF_QK_DIM == LANES and DIFF_V_DIM == LANES
    groups = DIFF_HEADS // hp
    w = hp * LANES
    grid_spec = pltpu.PrefetchScalarGridSpec(
        num_scalar_prefetch=2,
        grid=(bsz, groups, int(qi_tab.shape[0])),
        in_specs=[
            pl.BlockSpec((1, t, w), lambda b, h, p, qt, kt: (b, qt[p], h)),
            pl.BlockSpec((1, t, w), lambda b, h, p, qt, kt: (b, kt[p], groups + h)),
            pl.BlockSpec((1, t, w), lambda b, h, p, qt, kt: (b, kt[p], 2 * groups + h)),
            pl.BlockSpec((4, DIFF_QK_DIM), lambda b, h, p, qt, kt: (0, 0)),
            pl.BlockSpec((1, DIFF_V_DIM), lambda b, h, p, qt, kt: (0, 0)),
        ],
        out_specs=pl.BlockSpec((1, t, w), lambda b, h, p, qt, kt: (b, qt[p], h)),
        scratch_shapes=[pltpu.VMEM((hp * 2 * t, LANES), BF16),
                        pltpu.VMEM((hp * 2 * t, LANES), F32),
                        pltpu.VMEM((hp * 2 * t, 2 * DIFF_V_DIM), F32)],
    )
    return pl.pallas_call(
        functools.partial(_diff_attn_kernel, lambda_init=lambda_init),
        grid_spec=grid_spec,
        out_shape=jax.ShapeDtypeStruct((bsz, seq, DIFF_HEADS * DIFF_V_DIM), BF16),
        compiler_params=_params("arbitrary", "arbitrary", "arbitrary"),
        name="diff_attn",
    )(qi_tab, ki_tab, h3, h3, h3, lam, subln.reshape(1, DIFF_V_DIM))


def _moba_kernel(qi_tab, ki_tab, q_ref, k_ref, v_ref, km_ref, oh_ref, o_ref,
                 qa_scr, m_scr, acc_scr, *, topk):
    p_id = pl.program_id(2)
    qi = qi_tab[p_id]
    ki = ki_tab[p_id]
    tq = q_ref.shape[1]
    d = MOBA_HEAD_DIM
    heads = q_ref.shape[2] // d
    rb = ATTN_ROWS

    @pl.when(ki == 0)
    def _init():
        row = lax.broadcasted_iota(jnp.int32, (tq, 1), 0)
        own = (qi * tq + row) // MOBA_BLOCK
        for hh in range(heads):
            q = q_ref[0, :, hh * d:(hh + 1) * d]
            gate = _nt_dot(q.astype(F32), km_ref[0, :, hh * d:(hh + 1) * d],
                           precision=lax.Precision.HIGHEST)
            blk = lax.broadcasted_iota(jnp.int32, gate.shape, 1)
            g = jnp.where(blk < own, gate, -jnp.inf)
            sel = blk == own
            for _ in range(topk):
                mx = jnp.max(g, axis=-1, keepdims=True)
                idx = jnp.min(jnp.where(g == mx, blk, LANES), axis=-1, keepdims=True)
                sel = sel | ((blk == idx) & (mx > -jnp.inf))
                g = jnp.where(blk == idx, -jnp.inf, g)
            qa_scr[hh * tq:(hh + 1) * tq, 0:d] = q
            qa_scr[hh * tq:(hh + 1) * tq, d:2 * d] = jnp.where(sel, 0.0, NEG_BIG).astype(BF16)
        m_scr[...] = jnp.full(m_scr.shape, NEG_BIG, F32)
        acc_scr[...] = jnp.zeros(acc_scr.shape, F32)

    def step(diagonal):
        onehot = oh_ref[...]
        for hh in range(heads):
            k_aug = jnp.concatenate([k_ref[0, :, hh * d:(hh + 1) * d], onehot], axis=1)
            _attend_tile(qa_scr[hh * tq:(hh + 1) * tq, :], k_aug, v_ref[0, :, hh * d:(hh + 1) * d],
                         m_scr, acc_scr, hh * tq, tq, diagonal)

    @pl.when(ki < qi)
    def _past():
        step(False)

    @pl.when(ki == qi)
    def _diag():
        step(True)
        for hh in range(heads):
            rows = slice(hh * tq, (hh + 1) * tq)
            o_ref[0, :, hh * d:(hh + 1) * d] = (acc_scr[rows, 0:d] / acc_scr[rows, d:2 * d]).astype(o_ref.dtype)


def _moba_attention(h3, kmean):
    bsz, seq, _ = h3.shape
    t = ATTN_TILE
    nb = seq // MOBA_BLOCK
    assert seq % t == 0 and t % MOBA_BLOCK == 0 and nb <= LANES and MOBA_HEAD_DIM == LANES
    n = seq // t
    qi_tab, ki_tab = _causal_pairs(n)
    hp = ATTN_HEADS_PER_STEP
    assert MOBA_HEADS % hp == 0
    groups = MOBA_HEADS // hp
    w = hp * MOBA_HEAD_DIM
    q0 = (3 * DIFF_HEADS * DIFF_V_DIM) // w
    k0 = q0 + groups
    v0 = k0 + groups
    kmean = jnp.pad(kmean, ((0, 0), (0, LANES - nb), (0, 0)))
    block_of_key = jnp.arange(seq, dtype=jnp.int32)[:, None] // MOBA_BLOCK
    onehot = (block_of_key == jnp.arange(LANES, dtype=jnp.int32)[None, :]).astype(BF16)
    grid_spec = pltpu.PrefetchScalarGridSpec(
        num_scalar_prefetch=2,
        grid=(bsz, groups, int(qi_tab.shape[0])),
        in_specs=[
            pl.BlockSpec((1, t, w), lambda b, h, p, qt, kt: (b, qt[p], q0 + h)),
            pl.BlockSpec((1, t, w), lambda b, h, p, qt, kt: (b, kt[p], k0 + h)),
            pl.BlockSpec((1, t, w), lambda b, h, p, qt, kt: (b, kt[p], v0 + h)),
            pl.BlockSpec((1, LANES, w), lambda b, h, p, qt, kt: (b, 0, h)),
            pl.BlockSpec((t, LANES), lambda b, h, p, qt, kt: (kt[p], 0)),
        ],
        out_specs=pl.BlockSpec((1, t, w), lambda b, h, p, qt, kt: (b, qt[p], h)),
        scratch_shapes=[pltpu.VMEM((hp * t, 2 * MOBA_HEAD_DIM), BF16),
                        pltpu.VMEM((hp * t, LANES), F32),
                        pltpu.VMEM((hp * t, 2 * MOBA_HEAD_DIM), F32)],
    )
    return pl.pallas_call(
        functools.partial(_moba_kernel, topk=min(MOBA_TOPK, nb)),
        grid_spec=grid_spec,
        out_shape=jax.ShapeDtypeStruct((bsz, seq, MOBA_HEADS * MOBA_HEAD_DIM), BF16),
        compiler_params=_params("arbitrary", "arbitrary", "arbitrary"),
        name="moba_attn",
    )(qi_tab, ki_tab, h3, h3, h3, kmean, onehot)


def _hy_outproj_kernel(a_ref, m_ref, wa_ref, wm_ref, x_ref, g_ref, b_ref, o_ref, *, alpha):
    y = alpha * x_ref[...]
    y = y + jnp.dot(a_ref[...], wa_ref[...], preferred_element_type=F32)
    y = y + jnp.dot(m_ref[...], wm_ref[...], preferred_element_type=F32)
    o_ref[...] = _layer_norm_rows(y, g_ref[...], b_ref[...])


def _hy_outproj(a2, m2, w_out, x2, g, b, alpha):
    tokens, d = x2.shape
    wa = a2.shape[1]
    wm = m2.shape[1]
    row = lambda width: pl.BlockSpec((ROW_TILE, width), lambda i: (i, 0))
    return pl.pallas_call(
        functools.partial(_hy_outproj_kernel, alpha=alpha),
        grid=(tokens // ROW_TILE,),
        in_specs=[row(wa), row(wm), _const_spec((wa, d)), _const_spec((wm, d)), row(d),
                  _const_spec((1, d)), _const_spec((1, d))],
        out_specs=row(d),
        out_shape=jax.ShapeDtypeStruct((tokens, d), F32),
        compiler_params=_params("arbitrary"),
        name="hy_outproj_ln",
    )(a2, m2, w_out[:wa], w_out[wa:], x2, g.reshape(1, d), b.reshape(1, d))


def _mlp_kernel(x_ref, w1_ref, w2_ref, g_ref, b_ref, o_ref, *, alpha):
    x = x_ref[...]
    xb = x.astype(BF16)
    y = alpha * x
    for c in range(w1_ref.shape[1] // COL_CHUNK):
        cols = slice(c * COL_CHUNK, (c + 1) * COL_CHUNK)
        hid = jnp.maximum(jnp.dot(xb, w1_ref[:, cols], preferred_element_type=F32), 0.0)
        y = y + jnp.dot((hid * hid).astype(BF16), w2_ref[cols, :], preferred_element_type=F32)
    o_ref[...] = _layer_norm_rows(y, g_ref[...], b_ref[...])


def _mlp(x2, w1, w2, g, b, alpha):
    tokens, d = x2.shape
    dff = w1.shape[1]
    row = pl.BlockSpec((ROW_TILE, d), lambda i: (i, 0))
    return pl.pallas_call(
        functools.partial(_mlp_kernel, alpha=alpha),
        grid=(tokens // ROW_TILE,),
        in_specs=[row, _const_spec((d, dff)), _const_spec((dff, d)),
                  _const_spec((1, d)), _const_spec((1, d))],
        out_specs=row,
        out_shape=jax.ShapeDtypeStruct((tokens, d), F32),
        compiler_params=_params("arbitrary"),
        name="mlp_ln",
    )(x2, w1, w2, g.reshape(1, d), b.reshape(1, d))


def _gla_inproj_kernel(x_ref, w_ref, wd_ref, wu_ref, bg_ref, h_ref, g_ref):
    xb = x_ref[...].astype(BF16)
    for c in range(w_ref.shape[1] // COL_CHUNK):
        cols = slice(c * COL_CHUNK, (c + 1) * COL_CHUNK)
        h_ref[:, cols] = jnp.dot(xb, w_ref[:, cols], preferred_element_type=F32).astype(BF16)
    low = jnp.dot(xb, wd_ref[...], preferred_element_type=F32)
    z = jnp.dot(low.astype(BF16), wu_ref[...], preferred_element_type=F32) + bg_ref[...]
    log_sig = jnp.minimum(z, 0.0) - jnp.log1p(jnp.exp(-jnp.abs(z)))
    g_ref[...] = log_sig / GLA_GATE_NORM


def _gla_inproj(x2, w_main, w_down, w_up, b_gate):
    tokens, d = x2.shape
    width = w_main.shape[1]
    kw = w_up.shape[1]
    row = lambda w_: pl.BlockSpec((ROW_TILE, w_), lambda i: (i, 0))
    return pl.pallas_call(
        _gla_inproj_kernel,
        grid=(tokens // ROW_TILE,),
        in_specs=[row(d), _const_spec((d, width)), _const_spec((d, LANES)),
                  _const_spec((LANES, kw)), _const_spec((1, kw))],
        out_specs=[row(width), row(kw)],
        out_shape=[jax.ShapeDtypeStruct((tokens, width), BF16),
                   jax.ShapeDtypeStruct((tokens, kw), F32)],
        compiler_params=_params("arbitrary"),
        name="gla_inproj",
    )(x2, w_main, w_down, w_up, b_gate.reshape(1, kw))


def _gla_kernel(q_ref, k_ref, v_ref, g_ref, ng_ref, o_ref, st_ref, *, scale):
    L = GLA_CHUNK
    C = GLA_SUB
    n_sub = L // C
    n_chunks = q_ref.shape[1] // L
    hi = lax.Precision.HIGHEST

    @pl.when(pl.program_id(2) == 0)
    def _reset():
        st_ref[...] = jnp.zeros(st_ref.shape, F32)

    r_i = lax.broadcasted_iota(jnp.int32, (L, L), 0)
    c_i = lax.broadcasted_iota(jnp.int32, (L, L), 1)
    tri = (c_i <= r_i).astype(F32)
    ones = jnp.ones((L, GLA_DK), F32)
    sub_row = lax.broadcasted_iota(jnp.int32, (L, 1), 0) % C

    def chunk(ci, carry):
        rows = pl.ds(pl.multiple_of(ci * L, L), L)
        g = g_ref[0, rows, :]
        q = q_ref[0, rows, :].astype(F32) * scale
        k = k_ref[0, rows, :].astype(F32)
        v = v_ref[0, rows, :]
        vf = v.astype(F32)
        b = jnp.dot(tri, g, preferred_element_type=F32, precision=hi)
        state = st_ref[...]

        o = jnp.dot((q * jnp.exp(b)).astype(BF16), state.astype(BF16), preferred_element_type=F32)

        parts = [jnp.zeros((C, GLA_DV), F32)]
        for i in range(1, n_sub):
            ref_row = b[i * C - 1:i * C, :]
            qs = q[i * C:(i + 1) * C, :] * jnp.exp(b[i * C:(i + 1) * C, :] - ref_row)
            ks = k[0:i * C, :] * jnp.exp(ref_row - b[0:i * C, :])
            att = _nt_dot(qs.astype(BF16), ks.astype(BF16))
            parts.append(jnp.dot(att.astype(BF16), v[0:i * C, :], preferred_element_type=F32))
        o = o + jnp.concatenate(parts, axis=0)

        for d in range(C):
            if d == 0:
                kr, br, vr = k, b, vf
            else:
                kr = pltpu.roll(k, d, axis=0)
                br = pltpu.roll(b, d, axis=0)
                vr = pltpu.roll(vf, d, axis=0)
            w = jnp.sum(q * kr * jnp.exp(jnp.minimum(b - br, 0.0)), axis=-1, keepdims=True)
            o = o + jnp.where(sub_row >= d, w, 0.0) * vr

        y = o * lax.rsqrt(jnp.mean(o * o, axis=-1, keepdims=True) + RMS_EPS)
        o_ref[0, rows, :] = (y * ng_ref[...]).astype(o_ref.dtype)

        b_last = b[L - 1:L, :]
        kd = (k * jnp.exp(b_last - b)).astype(BF16)
        decay_log = _tn_dot(g, ones, precision=hi)
        decay = jnp.exp(decay_log)
        decay = jnp.concatenate([decay] * (GLA_DV // GLA_DK), axis=1)
        st_ref[...] = decay * state + _tn_dot(kd, v)
        return carry

    lax.fori_loop(0, n_chunks, chunk, 0)


def _gla_recurrence(h3, g3, norm_g):
    bsz, seq, _ = h3.shape
    t = GLA_STEP
    assert seq % t == 0
    kq = GLA_HEADS
    vb = (2 * GLA_HEADS * GLA_DK) // GLA_DV
    return pl.pallas_call(
        functools.partial(_gla_kernel, scale=GLA_DK ** -0.5),
        grid=(bsz, GLA_HEADS, seq // t),
        in_specs=[
            pl.BlockSpec((1, t, GLA_DK), lambda b, h, s: (b, s, h)),
            pl.BlockSpec((1, t, GLA_DK), lambda b, h, s: (b, s, kq + h)),
            pl.BlockSpec((1, t, GLA_DV), lambda b, h, s: (b, s, vb + h)),
            pl.BlockSpec((1, t, GLA_DK), lambda b, h, s: (b, s, h)),
            pl.BlockSpec((1, GLA_DV), lambda b, h, s: (0, 0)),
        ],
        out_specs=pl.BlockSpec((1, t, GLA_DV), lambda b, h, s: (b, s, h)),
        out_shape=jax.ShapeDtypeStruct((bsz, seq, GLA_HEADS * GLA_DV), BF16),
        scratch_shapes=[pltpu.VMEM((GLA_DK, GLA_DV), F32)],
        compiler_params=_params("arbitrary", "arbitrary", "arbitrary"),
        name="gla_recurrence",
    )(h3, h3, h3, g3, norm_g.reshape(1, GLA_DV))


def _gla_outproj_kernel(o_in_ref, r_ref, w_ref, x_ref, g_ref, b_ref, o_ref, *, alpha):
    r = r_ref[...].astype(F32)
    gated = o_in_ref[...].astype(F32) * (r * jax.nn.sigmoid(r))
    y = alpha * x_ref[...] + jnp.dot(gated.astype(BF16), w_ref[...], preferred_element_type=F32)
    o_ref[...] = _layer_norm_rows(y, g_ref[...], b_ref[...])


def _gla_outproj(o2, h2, w_out, x2, g, b, alpha):
    tokens, d = x2.shape
    vw = o2.shape[1]
    r_block = (h2.shape[1] - vw) // vw
    assert r_block * vw + vw == h2.shape[1]
    row = lambda width: pl.BlockSpec((ROW_TILE, width), lambda i: (i, 0))
    return pl.pallas_call(
        functools.partial(_gla_outproj_kernel, alpha=alpha),
        grid=(tokens // ROW_TILE,),
        in_specs=[row(vw), pl.BlockSpec((ROW_TILE, vw), lambda i: (i, r_block)),
                  _const_spec((vw, d)), row(d), _const_spec((1, d)), _const_spec((1, d))],
        out_specs=row(d),
        out_shape=jax.ShapeDtypeStruct((tokens, d), F32),
        compiler_params=_params("arbitrary"),
        name="gla_outproj_ln",
    )(o2, h2, w_out, x2, g.reshape(1, d), b.reshape(1, d))


def kernel(x, hy_w_in, diff_lambda, diff_subln, hy_w_out, gla_w_in, gla_w_gate_up, gla_b_gate,
           gla_norm, gla_w_out, ln_mix_g, ln_mix_b, ffn_w1, ffn_w2, ln_ffn_g, ln_ffn_b):
    bsz, seq, d = x.shape
    depth = ln_mix_g.shape[0]
    alpha = (2 * depth) ** 0.25
    tokens = bsz * seq
    assert tokens % ROW_TILE == 0

    tabs = _rope_tables(seq, DIFF_QK_DIM) + _rope_tables(seq, MOBA_HEAD_DIM)
    gla_main = 2 * GLA_HEADS * GLA_DK + 2 * GLA_HEADS * GLA_DV

    x2 = x.reshape(tokens, d)
    for layer in range(depth):
        if layer % 2 == 0:
            e = layer // 2
            lambda_init = 0.8 - 0.6 * math.exp(-0.3 * layer)
            h2, kmean = _hy_inproj(x2, hy_w_in[e].astype(BF16), tabs, seq)
            h3 = h2.reshape(bsz, seq, h2.shape[1])
            kmean = kmean.reshape(bsz, seq // MOBA_BLOCK, kmean.shape[-1])
            a = _diff_attention(h3, diff_lambda[e], diff_subln[e], lambda_init)
            m = _moba_attention(h3, kmean)
            x2 = _hy_outproj(a.reshape(tokens, -1), m.reshape(tokens, -1), hy_w_out[e].astype(BF16),
                             x2, ln_mix_g[layer], ln_mix_b[layer], alpha)
        else:
            o = layer // 2
            w_in = gla_w_in[o]
            w_down = jnp.pad(w_in[:, gla_main:], ((0, 0), (0, LANES - GLA_GATE_RANK))).astype(BF16)
            w_up = jnp.pad(gla_w_gate_up[o], ((0, LANES - GLA_GATE_RANK), (0, 0))).astype(BF16)
            h2, g2 = _gla_inproj(x2, w_in[:, :gla_main].astype(BF16), w_down, w_up, gla_b_gate[o])
            h3 = h2.reshape(bsz, seq, h2.shape[1])
            g3 = g2.reshape(bsz, seq, g2.shape[1])
            og = _gla_recurrence(h3, g3, gla_norm[o])
            x2 = _gla_outproj(og.reshape(tokens, -1), h2, gla_w_out[o].astype(BF16), x2,
                              ln_mix_g[layer], ln_mix_b[layer], alpha)
        x2 = _mlp(x2, ffn_w1[layer].astype(BF16), ffn_w2[layer].astype(BF16),
                  ln_ffn_g[layer], ln_ffn_b[layer], alpha)
    return x2.reshape(bsz, seq, d)
```

```python
import functools
import math

import jax
import jax.numpy as jnp
import numpy as np
from jax import lax
from jax.experimental import pallas as pl
from jax.experimental.pallas import tpu as pltpu

F32 = jnp.float32
BF16 = jnp.bfloat16

DIFF_HEADS = 4
DIFF_QK_DIM = 64
DIFF_V_DIM = 128
MOBA_HEADS = 4
MOBA_HEAD_DIM = 128
MOBA_BLOCK = 256
MOBA_TOPK = 3
GLA_HEADS = 4
GLA_DK = 128
GLA_DV = 256
GLA_GATE_RANK = 16
GLA_GATE_NORM = 16.0
ROPE_THETA = 10000.0
LN_EPS = 1e-5
RMS_EPS = 1e-5

LANES = 128
VMEM_LIMIT_BYTES = 56 * 1024 * 1024

ROW_TILE = 512
COL_CHUNK = 512
ATTN_TILE = 512
ATTN_ROWS = 128
ATTN_HEADS_PER_STEP = 4
LOG2E = math.log2(math.e)
GLA_CHUNK = 64
GLA_STEP = 512
NEG_BIG = -1e30


def _nt_dot(a, b, precision=None):
    return lax.dot_general(a, b, (((1,), (1,)), ((), ())),
                           preferred_element_type=F32, precision=precision)


def _tn_dot(a, b, precision=None):
    return lax.dot_general(a, b, (((0,), (0,)), ((), ())),
                           preferred_element_type=F32, precision=precision)


def _split3(x):
    x1 = x.astype(BF16)
    r1 = x - x1.astype(F32)
    x2 = r1.astype(BF16)
    x3 = (r1 - x2.astype(F32)).astype(BF16)
    return x1, x2, x3


def _const_spec(shape):
    nd = len(shape)
    return pl.BlockSpec(shape, lambda *_: (0,) * nd, pipeline_mode=pl.Buffered(1))


def _params(*sem):
    return pltpu.CompilerParams(dimension_semantics=sem, vmem_limit_bytes=VMEM_LIMIT_BYTES)


def _layer_norm_rows(y, g, b):
    mu = jnp.mean(y, axis=-1, keepdims=True)
    d = y - mu
    var = jnp.mean(d * d, axis=-1, keepdims=True)
    return d * lax.rsqrt(var + LN_EPS) * g + b


def _rope_tables(seq, group):
    half = group // 2
    inv = 1.0 / (ROPE_THETA ** (jnp.arange(0, group, 2, dtype=F32) / group))
    ang = jnp.arange(seq, dtype=F32)[:, None] * inv[None, :]
    cos, sin = jnp.cos(ang), jnp.sin(ang)
    cos_g = jnp.concatenate([cos, cos], axis=-1)
    sin_g = jnp.concatenate([-sin, sin], axis=-1)
    reps = LANES // group
    return jnp.tile(cos_g, (1, reps)), jnp.tile(sin_g, (1, reps))


def _rope(acc, cos, sin, group):
    width = acc.shape[1]
    half = group // 2
    lane = lax.broadcasted_iota(jnp.int32, acc.shape, 1)
    upper = pltpu.roll(acc, width - half, axis=1)
    lower = pltpu.roll(acc, half, axis=1)
    partner = jnp.where((lane % group) < half, upper, lower)
    reps = width // LANES
    c = jnp.concatenate([cos] * reps, axis=1)
    s = jnp.concatenate([sin] * reps, axis=1)
    return acc * c + partner * s


def _hy_inproj_kernel(x_ref, w_ref, cd_ref, sd_ref, cm_ref, sm_ref, h_ref, km_ref):
    xb = x_ref[...].astype(BF16)
    rows = xb.shape[0]
    n_chunks = w_ref.shape[1] // COL_CHUNK
    for c in range(n_chunks):
        cols = slice(c * COL_CHUNK, (c + 1) * COL_CHUNK)
        acc = jnp.dot(xb, w_ref[:, cols], preferred_element_type=F32)
        if c == 0:
            acc = _rope(acc, cd_ref[...], sd_ref[...], DIFF_QK_DIM) * (DIFF_QK_DIM ** -0.5 * LOG2E)
        elif c == 1:
            acc = _rope(acc, cd_ref[...], sd_ref[...], DIFF_QK_DIM)
        elif c == 3:
            acc = _rope(acc, cm_ref[...], sm_ref[...], MOBA_HEAD_DIM) * (MOBA_HEAD_DIM ** -0.5 * LOG2E)
        elif c == 4:
            acc = _rope(acc, cm_ref[...], sm_ref[...], MOBA_HEAD_DIM)
        if c == 4:
            blocks = rows // MOBA_BLOCK
            km_ref[0] = jnp.mean(acc.reshape(blocks, MOBA_BLOCK, COL_CHUNK), axis=1)
        h_ref[:, cols] = acc.astype(BF16)


def _hy_inproj(x2, w, tabs, seq):
    tokens, d = x2.shape
    width = w.shape[1]
    assert width == 6 * COL_CHUNK and seq % ROW_TILE == 0 and ROW_TILE % MOBA_BLOCK == 0
    steps = tokens // ROW_TILE
    per_seq = seq // ROW_TILE
    blocks = ROW_TILE // MOBA_BLOCK
    tab_spec = pl.BlockSpec((ROW_TILE, LANES), lambda i: (i % per_seq, 0))
    return pl.pallas_call(
        _hy_inproj_kernel,
        grid=(steps,),
        in_specs=[pl.BlockSpec((ROW_TILE, d), lambda i: (i, 0)),
                  _const_spec((d, width)),
                  tab_spec, tab_spec, tab_spec, tab_spec],
        out_specs=[pl.BlockSpec((ROW_TILE, width), lambda i: (i, 0)),
                   pl.BlockSpec((1, blocks, COL_CHUNK), lambda i: (i, 0, 0))],
        out_shape=[jax.ShapeDtypeStruct((tokens, width), BF16),
                   jax.ShapeDtypeStruct((steps, blocks, COL_CHUNK), F32)],
        compiler_params=_params("arbitrary"),
        name="hy_inproj",
    )(x2, w, *tabs)


def _causal_pairs(n):
    qi = np.array([q for q in range(n) for _ in range(q + 1)], np.int32)
    ki = np.array([k for q in range(n) for k in range(q + 1)], np.int32)
    return jnp.asarray(qi), jnp.asarray(ki)


def _with_ones(v):
    return jnp.concatenate([v, jnp.ones(v.shape, v.dtype)], axis=1)


def _flash_update(s, v_ext, m_ref, acc_ref, rows):
    reps = s.shape[1] // LANES
    m_prev = m_ref[rows, :]
    m_new = jnp.maximum(m_prev, jnp.max(s, axis=-1, keepdims=True))
    alpha = jnp.exp2(m_prev - m_new)
    p = jnp.exp2(s - jnp.concatenate([m_new] * reps, axis=1))
    pv = jnp.dot(p.astype(BF16), v_ext, preferred_element_type=F32)
    acc_ref[rows, :] = jnp.concatenate([alpha, alpha], axis=1) * acc_ref[rows, :] + pv
    m_ref[rows, :] = m_new


def _attend_tile(s_all, v, m_ref, acc_ref, row0, tq, diagonal):
    rb = ATTN_ROWS
    v_ext = _with_ones(v)
    for r in range(s_all.shape[0] // rb):
        rows = slice(row0 + r * rb, row0 + (r + 1) * rb)
        s = s_all[r * rb:(r + 1) * rb]
        if diagonal:
            first = (r * rb) % tq
            keys = first + rb
            row = lax.broadcasted_iota(jnp.int32, (rb, keys), 0) + first
            col = lax.broadcasted_iota(jnp.int32, (rb, keys), 1)
            s = jnp.where(col <= row, s[:, 0:keys], NEG_BIG)
            _flash_update(s, v_ext[0:keys], m_ref, acc_ref, rows)
        else:
            _flash_update(s, v_ext, m_ref, acc_ref, rows)


def _diff_attn_kernel(qi_tab, ki_tab, q_ref, k_ref, v_ref, lam_ref, sub_ref, o_ref,
                      qs_scr, m_scr, acc_scr, *, lambda_init):
    p_id = pl.program_id(2)
    qi = qi_tab[p_id]
    ki = ki_tab[p_id]
    tq = q_ref.shape[1]
    heads = q_ref.shape[2] // LANES
    dv = DIFF_V_DIM

    @pl.when(ki == 0)
    def _init():
        for hh in range(heads):
            q = q_ref[0, :, hh * LANES:(hh + 1) * LANES]
            lane = lax.broadcasted_iota(jnp.int32, q.shape, 1)
            zero = jnp.zeros_like(q)
            base = hh * 2 * tq
            qs_scr[base:base + tq, :] = jnp.where(lane < DIFF_QK_DIM, q, zero)
            qs_scr[base + tq:base + 2 * tq, :] = jnp.where(lane >= DIFF_QK_DIM, q, zero)
        m_scr[...] = jnp.full(m_scr.shape, NEG_BIG, F32)
        acc_scr[...] = jnp.zeros(acc_scr.shape, F32)

    def step(diagonal):
        scores = [_nt_dot(qs_scr[hh * 2 * tq:(hh + 1) * 2 * tq, :], k_ref[0, :, hh * LANES:(hh + 1) * LANES])
                  for hh in range(heads)]
        for hh in range(heads):
            _attend_tile(scores[hh], v_ref[0, :, hh * dv:(hh + 1) * dv], m_scr, acc_scr, hh * 2 * tq, tq,
                         diagonal)

    @pl.when(ki < qi)
    def _full():
        step(False)

    @pl.when(ki == qi)
    def _diag():
        step(True)
        lam = lam_ref[...].astype(F32)
        lam_full = (jnp.exp(jnp.sum(lam[0:1] * lam[1:2], axis=-1, keepdims=True))
                    - jnp.exp(jnp.sum(lam[2:3] * lam[3:4], axis=-1, keepdims=True))
                    + lambda_init)
        for hh in range(heads):
            base = hh * 2 * tq
            o1 = acc_scr[base:base + tq, 0:dv] / acc_scr[base:base + tq, dv:]
            o2 = acc_scr[base + tq:base + 2 * tq, 0:dv] / acc_scr[base + tq:base + 2 * tq, dv:]
            o = o1 - lam_full * o2
            y = o * lax.rsqrt(jnp.mean(o * o, axis=-1, keepdims=True) + RMS_EPS)
            o_ref[0, :, hh * dv:(hh + 1) * dv] = (y * sub_ref[...] * (1.0 - lambda_init)).astype(o_ref.dtype)


def _diff_attention(h3, lam, subln, lambda_init):
    bsz, seq, _ = h3.shape
    t = ATTN_TILE
    assert seq % t == 0
    n = seq // t
    qi_tab, ki_tab = _causal_pairs(n)
    hp = ATTN_HEADS_PER_STEP
    assert DIFF_HEADS % hp == 0 and 2 * DIFF_QK_DIM == LANES and DIFF_V_DIM == LANES
    groups = DIFF_HEADS // hp
    w = hp * LANES
    grid_spec = pltpu.PrefetchScalarGridSpec(
        num_scalar_prefetch=2,
        grid=(bsz, groups, int(qi_tab.shape[0])),
        in_specs=[
            pl.BlockSpec((1, t, w), lambda b, h, p, qt, kt: (b, qt[p], h)),
            pl.BlockSpec((1, t, w), lambda b, h, p, qt, kt: (b, kt[p], groups + h)),
            pl.BlockSpec((1, t, w), lambda b, h, p, qt, kt: (b, kt[p], 2 * groups + h)),
            pl.BlockSpec((4, DIFF_QK_DIM), lambda b, h, p, qt, kt: (0, 0)),
            pl.BlockSpec((1, DIFF_V_DIM), lambda b, h, p, qt, kt: (0, 0)),
        ],
        out_specs=pl.BlockSpec((1, t, w), lambda b, h, p, qt, kt: (b, qt[p], h)),
        scratch_shapes=[pltpu.VMEM((hp * 2 * t, LANES), BF16),
                        pltpu.VMEM((hp * 2 * t, LANES), F32),
                        pltpu.VMEM((hp * 2 * t, 2 * DIFF_V_DIM), F32)],
    )
    return pl.pallas_call(
        functools.partial(_diff_attn_kernel, lambda_init=lambda_init),
        grid_spec=grid_spec,
        out_shape=jax.ShapeDtypeStruct((bsz, seq, DIFF_HEADS * DIFF_V_DIM), BF16),
        compiler_params=_params("arbitrary", "arbitrary", "arbitrary"),
        name="diff_attn",
    )(qi_tab, ki_tab, h3, h3, h3, lam, subln.reshape(1, DIFF_V_DIM))


def _moba_kernel(qi_tab, ki_tab, q_ref, k_ref, v_ref, km_ref, oh_ref, o_ref,
                 qa_scr, m_scr, acc_scr, *, topk):
    p_id = pl.program_id(2)
    qi = qi_tab[p_id]
    ki = ki_tab[p_id]
    tq = q_ref.shape[1]
    d = MOBA_HEAD_DIM
    heads = q_ref.shape[2] // d

    @pl.when(ki == 0)
    def _init():
        nb = km_ref.shape[1]
        blk = lax.broadcasted_iota(jnp.int32, (nb, tq), 0)
        pos = lax.broadcasted_iota(jnp.int32, (nb, tq), 1)
        own = (qi * tq + pos) // MOBA_BLOCK
        for hh in range(heads):
            q = q_ref[0, :, hh * d:(hh + 1) * d]
            km3 = jnp.concatenate(_split3(km_ref[0, :, hh * d:(hh + 1) * d]), axis=0)
            g3 = _nt_dot(km3, q)
            gate = g3[0:nb] + g3[nb:2 * nb] + g3[2 * nb:3 * nb]
            g = jnp.where(blk < own, gate, -jnp.inf)
            sel = blk == own
            for _ in range(topk):
                mx = jnp.max(g, axis=0, keepdims=True)
                idx = jnp.min(jnp.where(g == mx, blk, nb), axis=0, keepdims=True)
                sel = sel | ((blk == idx) & (mx > -jnp.inf))
                g = jnp.where(blk == idx, -jnp.inf, g)
            bias_t = jnp.concatenate([jnp.where(sel, 0.0, NEG_BIG),
                                      jnp.full((LANES - nb, tq), NEG_BIG, F32)], axis=0)
            qa_scr[hh * tq:(hh + 1) * tq, 0:d] = q
            qa_scr[hh * tq:(hh + 1) * tq, d:2 * d] = jnp.transpose(bias_t).astype(BF16)
        m_scr[...] = jnp.full(m_scr.shape, NEG_BIG, F32)
        acc_scr[...] = jnp.zeros(acc_scr.shape, F32)

    def step(diagonal):
        onehot = oh_ref[...]
        scores = [_nt_dot(qa_scr[hh * tq:(hh + 1) * tq, :],
                          jnp.concatenate([k_ref[0, :, hh * d:(hh + 1) * d], onehot], axis=1))
                  for hh in range(heads)]
        for hh in range(heads):
            _attend_tile(scores[hh], v_ref[0, :, hh * d:(hh + 1) * d], m_scr, acc_scr, hh * tq, tq, diagonal)

    @pl.when(ki < qi)
    def _past():
        step(False)

    @pl.when(ki == qi)
    def _diag():
        step(True)
        for hh in range(heads):
            rows = slice(hh * tq, (hh + 1) * tq)
            o_ref[0, :, hh * d:(hh + 1) * d] = (acc_scr[rows, 0:d] / acc_scr[rows, d:2 * d]).astype(o_ref.dtype)


def _moba_attention(h3, kmean):
    bsz, seq, _ = h3.shape
    t = ATTN_TILE
    nb = seq // MOBA_BLOCK
    assert seq % t == 0 and t % MOBA_BLOCK == 0 and nb <= LANES and MOBA_HEAD_DIM == LANES
    n = seq // t
    qi_tab, ki_tab = _causal_pairs(n)
    hp = ATTN_HEADS_PER_STEP
    assert MOBA_HEADS % hp == 0
    groups = MOBA_HEADS // hp
    w = hp * MOBA_HEAD_DIM
    q0 = (3 * DIFF_HEADS * DIFF_V_DIM) // w
    k0 = q0 + groups
    v0 = k0 + groups
    block_of_key = jnp.arange(seq, dtype=jnp.int32)[:, None] // MOBA_BLOCK
    onehot = (block_of_key == jnp.arange(LANES, dtype=jnp.int32)[None, :]).astype(BF16)
    grid_spec = pltpu.PrefetchScalarGridSpec(
        num_scalar_prefetch=2,
        grid=(bsz, groups, int(qi_tab.shape[0])),
        in_specs=[
            pl.BlockSpec((1, t, w), lambda b, h, p, qt, kt: (b, qt[p], q0 + h)),
            pl.BlockSpec((1, t, w), lambda b, h, p, qt, kt: (b, kt[p], k0 + h)),
            pl.BlockSpec((1, t, w), lambda b, h, p, qt, kt: (b, kt[p], v0 + h)),
            pl.BlockSpec((1, nb, w), lambda b, h, p, qt, kt: (b, 0, h)),
            pl.BlockSpec((t, LANES), lambda b, h, p, qt, kt: (kt[p], 0)),
        ],
        out_specs=pl.BlockSpec((1, t, w), lambda b, h, p, qt, kt: (b, qt[p], h)),
        scratch_shapes=[pltpu.VMEM((hp * t, 2 * MOBA_HEAD_DIM), BF16),
                        pltpu.VMEM((hp * t, LANES), F32),
                        pltpu.VMEM((hp * t, 2 * MOBA_HEAD_DIM), F32)],
    )
    return pl.pallas_call(
        functools.partial(_moba_kernel, topk=min(MOBA_TOPK, nb)),
        grid_spec=grid_spec,
        out_shape=jax.ShapeDtypeStruct((bsz, seq, MOBA_HEADS * MOBA_HEAD_DIM), BF16),
        compiler_params=_params("arbitrary", "arbitrary", "arbitrary"),
        name="moba_attn",
    )(qi_tab, ki_tab, h3, h3, h3, kmean, onehot)


def _hy_outproj_kernel(a_ref, m_ref, wa_ref, wm_ref, x_ref, g_ref, b_ref, o_ref, *, alpha):
    y = alpha * x_ref[...]
    y = y + jnp.dot(a_ref[...], wa_ref[...], preferred_element_type=F32)
    y = y + jnp.dot(m_ref[...], wm_ref[...], preferred_element_type=F32)
    o_ref[...] = _layer_norm_rows(y, g_ref[...], b_ref[...])


def _hy_outproj(a2, m2, w_out, x2, g, b, alpha):
    tokens, d = x2.shape
    wa = a2.shape[1]
    wm = m2.shape[1]
    row = lambda width: pl.BlockSpec((ROW_TILE, width), lambda i: (i, 0))
    return pl.pallas_call(
        functools.partial(_hy_outproj_kernel, alpha=alpha),
        grid=(tokens // ROW_TILE,),
        in_specs=[row(wa), row(wm), _const_spec((wa, d)), _const_spec((wm, d)), row(d),
                  _const_spec((1, d)), _const_spec((1, d))],
        out_specs=row(d),
        out_shape=jax.ShapeDtypeStruct((tokens, d), F32),
        compiler_params=_params("arbitrary"),
        name="hy_outproj_ln",
    )(a2, m2, w_out[:wa], w_out[wa:], x2, g.reshape(1, d), b.reshape(1, d))


def _mlp_kernel(x_ref, w1_ref, w2_ref, g_ref, b_ref, o_ref, *, alpha):
    x = x_ref[...]
    xb = x.astype(BF16)
    y = alpha * x
    for c in range(w1_ref.shape[1] // COL_CHUNK):
        cols = slice(c * COL_CHUNK, (c + 1) * COL_CHUNK)
        hid = jnp.maximum(jnp.dot(xb, w1_ref[:, cols], preferred_element_type=F32), 0.0)
        y = y + jnp.dot((hid * hid).astype(BF16), w2_ref[cols, :], preferred_element_type=F32)
    o_ref[...] = _layer_norm_rows(y, g_ref[...], b_ref[...])


def _mlp(x2, w1, w2, g, b, alpha):
    tokens, d = x2.shape
    dff = w1.shape[1]
    row = pl.BlockSpec((ROW_TILE, d), lambda i: (i, 0))
    return pl.pallas_call(
        functools.partial(_mlp_kernel, alpha=alpha),
        grid=(tokens // ROW_TILE,),
        in_specs=[row, _const_spec((d, dff)), _const_spec((dff, d)),
                  _const_spec((1, d)), _const_spec((1, d))],
        out_specs=row,
        out_shape=jax.ShapeDtypeStruct((tokens, d), F32),
        compiler_params=_params("arbitrary"),
        name="mlp_ln",
    )(x2, w1, w2, g.reshape(1, d), b.reshape(1, d))


def _gla_inproj_kernel(x_ref, w_ref, wd_ref, wu_ref, bg_ref, h_ref, g_ref):
    xb = x_ref[...].astype(BF16)
    for c in range(w_ref.shape[1] // COL_CHUNK):
        cols = slice(c * COL_CHUNK, (c + 1) * COL_CHUNK)
        h_ref[:, cols] = jnp.dot(xb, w_ref[:, cols], preferred_element_type=F32).astype(BF16)
    low = jnp.dot(xb, wd_ref[...], preferred_element_type=F32)
    z = jnp.dot(low.astype(BF16), wu_ref[...], preferred_element_type=F32) + bg_ref[...]
    log_sig = jnp.minimum(z, 0.0) - jnp.log1p(jnp.exp(-jnp.abs(z)))
    g_ref[...] = log_sig / GLA_GATE_NORM


def _gla_inproj(x2, w_main, w_down, w_up, b_gate):
    tokens, d = x2.shape
    width = w_main.shape[1]
    kw = w_up.shape[1]
    row = lambda w_: pl.BlockSpec((ROW_TILE, w_), lambda i: (i, 0))
    return pl.pallas_call(
        _gla_inproj_kernel,
        grid=(tokens // ROW_TILE,),
        in_specs=[row(d), _const_spec((d, width)), _const_spec((d, LANES)),
                  _const_spec((LANES, kw)), _const_spec((1, kw))],
        out_specs=[row(width), row(kw)],
        out_shape=[jax.ShapeDtypeStruct((tokens, width), BF16),
                   jax.ShapeDtypeStruct((tokens, kw), F32)],
        compiler_params=_params("arbitrary"),
        name="gla_inproj",
    )(x2, w_main, w_down, w_up, b_gate.reshape(1, kw))


def _gla_level_ref(row_of, dk, half, rows):
    span = 2 * half
    pieces = []
    if span >= 8:
        for p in range(rows // span):
            r = p * span + half - 1
            pieces.append(jnp.broadcast_to(row_of(r), (span, dk)))
    else:
        sub = lax.broadcasted_iota(jnp.int32, (8, dk), 0)
        for base in range(0, rows, 8):
            piece = None
            for p in range(8 // span):
                r = base + p * span + half - 1
                bc = jnp.broadcast_to(row_of(r), (8, dk))
                piece = bc if piece is None else jnp.where(sub >= p * span, bc, piece)
            pieces.append(piece)
    return jnp.concatenate(pieces, axis=0)


def _gla_kernel(q_ref, k_ref, v_ref, g_ref, ng_ref, o_ref, st_ref, b_scr, *, scale):
    L = GLA_CHUNK
    dk, dv, heads = GLA_DK, GLA_DV, GLA_HEADS
    n_chunks = q_ref.shape[1] // L
    levels = [L >> (i + 1) for i in range(L.bit_length() - 1)]

    @pl.when(pl.program_id(1) == 0)
    def _reset():
        st_ref[...] = jnp.zeros(st_ref.shape, F32)

    r_i = lax.broadcasted_iota(jnp.int32, (L, L), 0)
    c_i = lax.broadcasted_iota(jnp.int32, (L, L), 1)
    tri = (c_i <= r_i).astype(BF16)
    level_of = jnp.where(c_i == r_i, -1, -2)
    for idx, half in enumerate(levels):
        span = 2 * half
        member = ((r_i // span) == (c_i // span)) & ((r_i % span) >= half) & ((c_i % span) < half)
        level_of = jnp.where(member, idx, level_of)

    def chunk(ci, carry):
        rows = pl.ds(pl.multiple_of(ci * L, L), L)
        hs = range(heads)
        kcols = [slice(h * dk, (h + 1) * dk) for h in hs]
        vcols = [slice(h * dv, (h + 1) * dv) for h in hs]
        csum = [jnp.dot(tri, jnp.concatenate(_split3(g_ref[0, rows, kcols[h]]), axis=1),
                        preferred_element_type=F32) for h in hs]
        b = [c[:, 0:dk] + c[:, dk:2 * dk] + c[:, 2 * dk:3 * dk] for c in csum]
        for h in hs:
            b_scr[h] = b[h]
        qb = [q_ref[0, rows, kcols[h]] for h in hs]
        kb = [k_ref[0, rows, kcols[h]] for h in hs]
        v = [v_ref[0, rows, vcols[h]] for h in hs]
        qf = [qb[h].astype(F32) * scale for h in hs]
        kf = [kb[h].astype(F32) for h in hs]
        state = [st_ref[h] for h in hs]

        o = [jnp.dot((qf[h] * jnp.exp(b[h])).astype(BF16), state[h].astype(BF16),
                     preferred_element_type=F32) for h in hs]
        att = [jnp.where(level_of == -1, _nt_dot(qb[h], kb[h]) * scale, 0.0) for h in hs]
        for idx, half in enumerate(levels):
            for h in hs:
                ref = _gla_level_ref(lambda r, h=h: b_scr[h, r:r + 1, :], dk, half, L)
                e = jnp.exp(-jnp.abs(b[h] - ref))
                a = _nt_dot((qf[h] * e).astype(BF16), (kf[h] * e).astype(BF16))
                att[h] = jnp.where(level_of == idx, a, att[h])

        for h in hs:
            b_last = b_scr[h, L - 1:L, :]
            kd = (kf[h] * jnp.exp(b_last - b[h])).astype(BF16)
            decay = jnp.transpose(jnp.broadcast_to(jnp.exp(b_last), (dk, dk)))
            decay = jnp.concatenate([decay] * (dv // dk), axis=1)
            st_ref[h] = decay * state[h] + _tn_dot(kd, v[h])

        for h in hs:
            oh = o[h] + jnp.dot(att[h].astype(BF16), v[h], preferred_element_type=F32)
            y = oh * lax.rsqrt(jnp.mean(oh * oh, axis=-1, keepdims=True) + RMS_EPS)
            o_ref[0, rows, vcols[h]] = (y * ng_ref[...]).astype(o_ref.dtype)
        return carry

    lax.fori_loop(0, n_chunks, chunk, 0)


def _gla_recurrence(h3, g3, norm_g):
    bsz, seq, _ = h3.shape
    t = GLA_STEP
    assert seq % t == 0 and t % GLA_CHUNK == 0 and GLA_DV % GLA_DK == 0
    kw = GLA_HEADS * GLA_DK
    vw = GLA_HEADS * GLA_DV
    assert (2 * kw) % vw == 0
    return pl.pallas_call(
        functools.partial(_gla_kernel, scale=GLA_DK ** -0.5),
        grid=(bsz, seq // t),
        in_specs=[
            pl.BlockSpec((1, t, kw), lambda b, s: (b, s, 0)),
            pl.BlockSpec((1, t, kw), lambda b, s: (b, s, 1)),
            pl.BlockSpec((1, t, vw), lambda b, s: (b, s, (2 * kw) // vw)),
            pl.BlockSpec((1, t, kw), lambda b, s: (b, s, 0)),
            pl.BlockSpec((1, GLA_DV), lambda b, s: (0, 0)),
        ],
        out_specs=pl.BlockSpec((1, t, vw), lambda b, s: (b, s, 0)),
        out_shape=jax.ShapeDtypeStruct((bsz, seq, vw), BF16),
        scratch_shapes=[pltpu.VMEM((GLA_HEADS, GLA_DK, GLA_DV), F32),
                        pltpu.VMEM((GLA_HEADS, GLA_CHUNK, GLA_DK), F32)],
        compiler_params=_params("arbitrary", "arbitrary"),
        name="gla_recurrence",
    )(h3, h3, h3, g3, norm_g.reshape(1, GLA_DV))


def _gla_outproj_kernel(o_in_ref, r_ref, w_ref, x_ref, g_ref, b_ref, o_ref, *, alpha):
    r = r_ref[...].astype(F32)
    gated = o_in_ref[...].astype(F32) * (r * jax.nn.sigmoid(r))
    y = alpha * x_ref[...] + jnp.dot(gated.astype(BF16), w_ref[...], preferred_element_type=F32)
    o_ref[...] = _layer_norm_rows(y, g_ref[...], b_ref[...])


def _gla_outproj(o2, h2, w_out, x2, g, b, alpha):
    tokens, d = x2.shape
    vw = o2.shape[1]
    r_block = (h2.shape[1] - vw) // vw
    assert r_block * vw + vw == h2.shape[1]
    row = lambda width: pl.BlockSpec((ROW_TILE, width), lambda i: (i, 0))
    return pl.pallas_call(
        functools.partial(_gla_outproj_kernel, alpha=alpha),
        grid=(tokens // ROW_TILE,),
        in_specs=[row(vw), pl.BlockSpec((ROW_TILE, vw), lambda i: (i, r_block)),
                  _const_spec((vw, d)), row(d), _const_spec((1, d)), _const_spec((1, d))],
        out_specs=row(d),
        out_shape=jax.ShapeDtypeStruct((tokens, d), F32),
        compiler_params=_params("arbitrary"),
        name="gla_outproj_ln",
    )(o2, h2, w_out, x2, g.reshape(1, d), b.reshape(1, d))


def kernel(x, hy_w_in, diff_lambda, diff_subln, hy_w_out, gla_w_in, gla_w_gate_up, gla_b_gate,
           gla_norm, gla_w_out, ln_mix_g, ln_mix_b, ffn_w1, ffn_w2, ln_ffn_g, ln_ffn_b):
    bsz, seq, d = x.shape
    depth = ln_mix_g.shape[0]
    alpha = (2 * depth) ** 0.25
    tokens = bsz * seq
    assert tokens % ROW_TILE == 0

    tabs = _rope_tables(seq, DIFF_QK_DIM) + _rope_tables(seq, MOBA_HEAD_DIM)
    gla_main = 2 * GLA_HEADS * GLA_DK + 2 * GLA_HEADS * GLA_DV

    x2 = x.reshape(tokens, d)
    for layer in range(depth):
        if layer % 2 == 0:
            e = layer // 2
            lambda_init = 0.8 - 0.6 * math.exp(-0.3 * layer)
            h2, kmean = _hy_inproj(x2, hy_w_in[e].astype(BF16), tabs, seq)
            h3 = h2.reshape(bsz, seq, h2.shape[1])
            kmean = kmean.reshape(bsz, seq // MOBA_BLOCK, kmean.shape[-1])
            a = _diff_attention(h3, diff_lambda[e], diff_subln[e], lambda_init)
            m = _moba_attention(h3, kmean)
            x2 = _hy_outproj(a.reshape(tokens, -1), m.reshape(tokens, -1), hy_w_out[e].astype(BF16),
                             x2, ln_mix_g[layer], ln_mix_b[layer], alpha)
        else:
            o = layer // 2
            w_in = gla_w_in[o]
            w_down = jnp.pad(w_in[:, gla_main:], ((0, 0), (0, LANES - GLA_GATE_RANK))).astype(BF16)
            w_up = jnp.pad(gla_w_gate_up[o], ((0, LANES - GLA_GATE_RANK), (0, 0))).astype(BF16)
            h2, g2 = _gla_inproj(x2, w_in[:, :gla_main].astype(BF16), w_down, w_up, gla_b_gate[o])
            h3 = h2.reshape(bsz, seq, h2.shape[1])
            g3 = g2.reshape(bsz, seq, g2.shape[1])
            og = _gla_recurrence(h3, g3, gla_norm[o])
            x2 = _gla_outproj(og.reshape(tokens, -1), h2, gla_w_out[o].astype(BF16), x2,
                              ln_mix_g[layer], ln_mix_b[layer], alpha)
        x2 = _mlp(x2, ffn_w1[layer].astype(BF16), ffn_w2[layer].astype(BF16),
                  ln_ffn_g[layer], ln_ffn_b[layer], alpha)
    return x2.reshape(bsz, seq, d)
```

```python
import functools
import math

import jax
import jax.numpy as jnp
import numpy as np
from jax import lax
from jax.experimental import pallas as pl
from jax.experimental.pallas import tpu as pltpu

F32 = jnp.float32
BF16 = jnp.bfloat16

DIFF_HEADS = 4
DIFF_QK_DIM = 64
DIFF_V_DIM = 128
MOBA_HEADS = 4
MOBA_HEAD_DIM = 128
MOBA_BLOCK = 256
MOBA_TOPK = 3
GLA_HEADS = 4
GLA_DK = 128
GLA_DV = 256
GLA_GATE_RANK = 16
GLA_GATE_NORM = 16.0
ROPE_THETA = 10000.0
LN_EPS = 1e-5
RMS_EPS = 1e-5

LANES = 128
VMEM_LIMIT_BYTES = 56 * 1024 * 1024

ROW_TILE = 512
COL_CHUNK = 512
ATTN_TILE = 512
ATTN_KEY_TILE = 1024
ATTN_ROWS = 128
ATTN_HEADS_PER_STEP = 4
LOG2E = math.log2(math.e)
GLA_CHUNK = 64
GLA_STEP = 512
NEG_BIG = -1e30


def _nt_dot(a, b, precision=None):
    return lax.dot_general(a, b, (((1,), (1,)), ((), ())),
                           preferred_element_type=F32, precision=precision)


def _tn_dot(a, b, precision=None):
    return lax.dot_general(a, b, (((0,), (0,)), ((), ())),
                           preferred_element_type=F32, precision=precision)


def _split3(x):
    x1 = x.astype(BF16)
    r1 = x - x1.astype(F32)
    x2 = r1.astype(BF16)
    x3 = (r1 - x2.astype(F32)).astype(BF16)
    return x1, x2, x3


def _const_spec(shape):
    nd = len(shape)
    return pl.BlockSpec(shape, lambda *_: (0,) * nd, pipeline_mode=pl.Buffered(1))


def _params(*sem):
    return pltpu.CompilerParams(dimension_semantics=sem, vmem_limit_bytes=VMEM_LIMIT_BYTES)


def _layer_norm_rows(y, g, b):
    mu = jnp.mean(y, axis=-1, keepdims=True)
    d = y - mu
    var = jnp.mean(d * d, axis=-1, keepdims=True)
    return d * lax.rsqrt(var + LN_EPS) * g + b


def _rope_tables(seq, group):
    half = group // 2
    inv = 1.0 / (ROPE_THETA ** (jnp.arange(0, group, 2, dtype=F32) / group))
    ang = jnp.arange(seq, dtype=F32)[:, None] * inv[None, :]
    cos, sin = jnp.cos(ang), jnp.sin(ang)
    cos_g = jnp.concatenate([cos, cos], axis=-1)
    sin_g = jnp.concatenate([-sin, sin], axis=-1)
    reps = LANES // group
    return jnp.tile(cos_g, (1, reps)), jnp.tile(sin_g, (1, reps))


def _rope(acc, cos, sin, group):
    width = acc.shape[1]
    half = group // 2
    lane = lax.broadcasted_iota(jnp.int32, acc.shape, 1)
    upper = pltpu.roll(acc, width - half, axis=1)
    lower = pltpu.roll(acc, half, axis=1)
    partner = jnp.where((lane % group) < half, upper, lower)
    reps = width // LANES
    c = jnp.concatenate([cos] * reps, axis=1)
    s = jnp.concatenate([sin] * reps, axis=1)
    return acc * c + partner * s


def _hy_inproj_kernel(x_ref, w_ref, cd_ref, sd_ref, cm_ref, sm_ref, h_ref, km_ref):
    xb = x_ref[...].astype(BF16)
    rows = xb.shape[0]
    n_chunks = w_ref.shape[1] // COL_CHUNK
    for c in range(n_chunks):
        cols = slice(c * COL_CHUNK, (c + 1) * COL_CHUNK)
        acc = jnp.dot(xb, w_ref[:, cols], preferred_element_type=F32)
        if c == 0:
            acc = _rope(acc, cd_ref[...], sd_ref[...], DIFF_QK_DIM) * (DIFF_QK_DIM ** -0.5 * LOG2E)
        elif c == 1:
            acc = _rope(acc, cd_ref[...], sd_ref[...], DIFF_QK_DIM)
        elif c == 3:
            acc = _rope(acc, cm_ref[...], sm_ref[...], MOBA_HEAD_DIM) * (MOBA_HEAD_DIM ** -0.5 * LOG2E)
        elif c == 4:
            acc = _rope(acc, cm_ref[...], sm_ref[...], MOBA_HEAD_DIM)
        if c == 4:
            blocks = rows // MOBA_BLOCK
            km_ref[0] = jnp.mean(acc.reshape(blocks, MOBA_BLOCK, COL_CHUNK), axis=1)
        h_ref[:, cols] = acc.astype(BF16)


def _hy_inproj(x2, w, tabs, seq):
    tokens, d = x2.shape
    width = w.shape[1]
    assert width == 6 * COL_CHUNK and seq % ROW_TILE == 0 and ROW_TILE % MOBA_BLOCK == 0
    steps = tokens // ROW_TILE
    per_seq = seq // ROW_TILE
    blocks = ROW_TILE // MOBA_BLOCK
    tab_spec = pl.BlockSpec((ROW_TILE, LANES), lambda i: (i % per_seq, 0))
    return pl.pallas_call(
        _hy_inproj_kernel,
        grid=(steps,),
        in_specs=[pl.BlockSpec((ROW_TILE, d), lambda i: (i, 0)),
                  _const_spec((d, width)),
                  tab_spec, tab_spec, tab_spec, tab_spec],
        out_specs=[pl.BlockSpec((ROW_TILE, width), lambda i: (i, 0)),
                   pl.BlockSpec((1, blocks, COL_CHUNK), lambda i: (i, 0, 0))],
        out_shape=[jax.ShapeDtypeStruct((tokens, width), BF16),
                   jax.ShapeDtypeStruct((steps, blocks, COL_CHUNK), F32)],
        compiler_params=_params("arbitrary"),
        name="hy_inproj",
    )(x2, w, *tabs)


def _causal_pairs(n_q, ratio):
    qi = np.array([q for q in range(n_q) for _ in range(q // ratio + 1)], np.int32)
    ki = np.array([k for q in range(n_q) for k in range(q // ratio + 1)], np.int32)
    return jnp.asarray(qi), jnp.asarray(ki)


def _for_each_tile_kind(qi, ki, tq, tk, step, finish):
    ratio = tk // tq

    @pl.when(ki < qi // ratio)
    def _full():
        step(None)

    for part in range(ratio):
        @pl.when((ki == qi // ratio) & (qi % ratio == part))
        def _diag(part=part):
            step(part * tq)
            finish()


def _with_ones(v):
    return jnp.concatenate([v, jnp.ones(v.shape, v.dtype)], axis=1)


def _flash_update(s, v_ext, m_ref, acc_ref, rows):
    reps = s.shape[1] // LANES
    m_prev = m_ref[rows, :]
    m_new = jnp.maximum(m_prev, jnp.max(s, axis=-1, keepdims=True))
    alpha = jnp.exp2(m_prev - m_new)
    p = jnp.exp2(s - jnp.concatenate([m_new] * reps, axis=1))
    pv = jnp.dot(p.astype(BF16), v_ext, preferred_element_type=F32)
    acc_ref[rows, :] = jnp.concatenate([alpha, alpha], axis=1) * acc_ref[rows, :] + pv
    m_ref[rows, :] = m_new


def _attend_tile(s_all, v, m_ref, acc_ref, row0, tq, diag_offset):
    rb = ATTN_ROWS
    v_ext = _with_ones(v)
    for r in range(s_all.shape[0] // rb):
        rows = slice(row0 + r * rb, row0 + (r + 1) * rb)
        s = s_all[r * rb:(r + 1) * rb]
        if diag_offset is not None:
            first = diag_offset + (r * rb) % tq
            keys = first + rb
            row = lax.broadcasted_iota(jnp.int32, (rb, keys), 0) + first
            col = lax.broadcasted_iota(jnp.int32, (rb, keys), 1)
            s = jnp.where(col <= row, s[:, 0:keys], NEG_BIG)
            _flash_update(s, v_ext[0:keys], m_ref, acc_ref, rows)
        else:
            _flash_update(s, v_ext, m_ref, acc_ref, rows)


def _diff_attn_kernel(qi_tab, ki_tab, q_ref, k_ref, v_ref, lam_ref, sub_ref, o_ref,
                      qs_scr, m_scr, acc_scr, *, lambda_init):
    p_id = pl.program_id(2)
    qi = qi_tab[p_id]
    ki = ki_tab[p_id]
    tq = q_ref.shape[1]
    heads = q_ref.shape[2] // LANES
    dv = DIFF_V_DIM

    @pl.when(ki == 0)
    def _init():
        for hh in range(heads):
            q = q_ref[0, :, hh * LANES:(hh + 1) * LANES]
            lane = lax.broadcasted_iota(jnp.int32, q.shape, 1)
            zero = jnp.zeros_like(q)
            base = hh * 2 * tq
            qs_scr[base:base + tq, :] = jnp.where(lane < DIFF_QK_DIM, q, zero)
            qs_scr[base + tq:base + 2 * tq, :] = jnp.where(lane >= DIFF_QK_DIM, q, zero)
        m_scr[...] = jnp.full(m_scr.shape, NEG_BIG, F32)
        acc_scr[...] = jnp.zeros(acc_scr.shape, F32)

    def step(diag_offset):
        scores = [_nt_dot(qs_scr[hh * 2 * tq:(hh + 1) * 2 * tq, :], k_ref[0, :, hh * LANES:(hh + 1) * LANES])
                  for hh in range(heads)]
        for hh in range(heads):
            _attend_tile(scores[hh], v_ref[0, :, hh * dv:(hh + 1) * dv], m_scr, acc_scr, hh * 2 * tq, tq,
                         diag_offset)

    def finish():
        lam = lam_ref[...].astype(F32)
        lam_full = (jnp.exp(jnp.sum(lam[0:1] * lam[1:2], axis=-1, keepdims=True))
                    - jnp.exp(jnp.sum(lam[2:3] * lam[3:4], axis=-1, keepdims=True))
                    + lambda_init)
        for hh in range(heads):
            base = hh * 2 * tq
            o1 = acc_scr[base:base + tq, 0:dv] / acc_scr[base:base + tq, dv:]
            o2 = acc_scr[base + tq:base + 2 * tq, 0:dv] / acc_scr[base + tq:base + 2 * tq, dv:]
            o = o1 - lam_full * o2
            y = o * lax.rsqrt(jnp.mean(o * o, axis=-1, keepdims=True) + RMS_EPS)
            o_ref[0, :, hh * dv:(hh + 1) * dv] = (y * sub_ref[...] * (1.0 - lambda_init)).astype(o_ref.dtype)

    _for_each_tile_kind(qi, ki, tq, k_ref.shape[1], step, finish)


def _diff_attention(h3, lam, subln, lambda_init):
    bsz, seq, _ = h3.shape
    t = ATTN_TILE
    tk = ATTN_KEY_TILE
    assert seq % tk == 0 and tk % t == 0
    qi_tab, ki_tab = _causal_pairs(seq // t, tk // t)
    hp = ATTN_HEADS_PER_STEP
    assert DIFF_HEADS % hp == 0 and 2 * DIFF_QK_DIM == LANES and DIFF_V_DIM == LANES
    groups = DIFF_HEADS // hp
    w = hp * LANES
    grid_spec = pltpu.PrefetchScalarGridSpec(
        num_scalar_prefetch=2,
        grid=(bsz, groups, int(qi_tab.shape[0])),
        in_specs=[
            pl.BlockSpec((1, t, w), lambda b, h, p, qt, kt: (b, qt[p], h)),
            pl.BlockSpec((1, tk, w), lambda b, h, p, qt, kt: (b, kt[p], groups + h)),
            pl.BlockSpec((1, tk, w), lambda b, h, p, qt, kt: (b, kt[p], 2 * groups + h)),
            pl.BlockSpec((4, DIFF_QK_DIM), lambda b, h, p, qt, kt: (0, 0)),
            pl.BlockSpec((1, DIFF_V_DIM), lambda b, h, p, qt, kt: (0, 0)),
        ],
        out_specs=pl.BlockSpec((1, t, w), lambda b, h, p, qt, kt: (b, qt[p], h)),
        scratch_shapes=[pltpu.VMEM((hp * 2 * t, LANES), BF16),
                        pltpu.VMEM((hp * 2 * t, LANES), F32),
                        pltpu.VMEM((hp * 2 * t, 2 * DIFF_V_DIM), F32)],
    )
    return pl.pallas_call(
        functools.partial(_diff_attn_kernel, lambda_init=lambda_init),
        grid_spec=grid_spec,
        out_shape=jax.ShapeDtypeStruct((bsz, seq, DIFF_HEADS * DIFF_V_DIM), BF16),
        compiler_params=_params("arbitrary", "arbitrary", "arbitrary"),
        name="diff_attn",
    )(qi_tab, ki_tab, h3, h3, h3, lam, subln.reshape(1, DIFF_V_DIM))


def _moba_kernel(qi_tab, ki_tab, q_ref, k_ref, v_ref, km_ref, oh_ref, o_ref,
                 qa_scr, m_scr, acc_scr, *, topk):
    p_id = pl.program_id(2)
    qi = qi_tab[p_id]
    ki = ki_tab[p_id]
    tq = q_ref.shape[1]
    d = MOBA_HEAD_DIM
    heads = q_ref.shape[2] // d

    @pl.when(ki == 0)
    def _init():
        nb = km_ref.shape[1]
        blk = lax.broadcasted_iota(jnp.int32, (nb, tq), 0)
        pos = lax.broadcasted_iota(jnp.int32, (nb, tq), 1)
        own = (qi * tq + pos) // MOBA_BLOCK
        for hh in range(heads):
            q = q_ref[0, :, hh * d:(hh + 1) * d]
            km3 = jnp.concatenate(_split3(km_ref[0, :, hh * d:(hh + 1) * d]), axis=0)
            g3 = _nt_dot(km3, q)
            gate = g3[0:nb] + g3[nb:2 * nb] + g3[2 * nb:3 * nb]
            g = jnp.where(blk < own, gate, -jnp.inf)
            sel = blk == own
            for _ in range(topk):
                mx = jnp.max(g, axis=0, keepdims=True)
                idx = jnp.min(jnp.where(g == mx, blk, nb), axis=0, keepdims=True)
                sel = sel | ((blk == idx) & (mx > -jnp.inf))
                g = jnp.where(blk == idx, -jnp.inf, g)
            bias_t = jnp.concatenate([jnp.where(sel, 0.0, NEG_BIG),
                                      jnp.full((LANES - nb, tq), NEG_BIG, F32)], axis=0)
            qa_scr[hh * tq:(hh + 1) * tq, 0:d] = q
            qa_scr[hh * tq:(hh + 1) * tq, d:2 * d] = jnp.transpose(bias_t).astype(BF16)
        m_scr[...] = jnp.full(m_scr.shape, NEG_BIG, F32)
        acc_scr[...] = jnp.zeros(acc_scr.shape, F32)

    def step(diag_offset):
        onehot = oh_ref[...]
        scores = [_nt_dot(qa_scr[hh * tq:(hh + 1) * tq, :],
                          jnp.concatenate([k_ref[0, :, hh * d:(hh + 1) * d], onehot], axis=1))
                  for hh in range(heads)]
        for hh in range(heads):
            _attend_tile(scores[hh], v_ref[0, :, hh * d:(hh + 1) * d], m_scr, acc_scr, hh * tq, tq, diag_offset)

    def finish():
        for hh in range(heads):
            rows = slice(hh * tq, (hh + 1) * tq)
            o_ref[0, :, hh * d:(hh + 1) * d] = (acc_scr[rows, 0:d] / acc_scr[rows, d:2 * d]).astype(o_ref.dtype)

    _for_each_tile_kind(qi, ki, tq, k_ref.shape[1], step, finish)


def _moba_attention(h3, kmean):
    bsz, seq, _ = h3.shape
    t = ATTN_TILE
    tk = ATTN_KEY_TILE
    nb = seq // MOBA_BLOCK
    assert seq % tk == 0 and tk % t == 0 and t % MOBA_BLOCK == 0 and nb <= LANES and MOBA_HEAD_DIM == LANES
    qi_tab, ki_tab = _causal_pairs(seq // t, tk // t)
    hp = ATTN_HEADS_PER_STEP
    assert MOBA_HEADS % hp == 0
    groups = MOBA_HEADS // hp
    w = hp * MOBA_HEAD_DIM
    q0 = (3 * DIFF_HEADS * DIFF_V_DIM) // w
    k0 = q0 + groups
    v0 = k0 + groups
    block_of_key = jnp.arange(seq, dtype=jnp.int32)[:, None] // MOBA_BLOCK
    onehot = (block_of_key == jnp.arange(LANES, dtype=jnp.int32)[None, :]).astype(BF16)
    grid_spec = pltpu.PrefetchScalarGridSpec(
        num_scalar_prefetch=2,
        grid=(bsz, groups, int(qi_tab.shape[0])),
        in_specs=[
            pl.BlockSpec((1, t, w), lambda b, h, p, qt, kt: (b, qt[p], q0 + h)),
            pl.BlockSpec((1, tk, w), lambda b, h, p, qt, kt: (b, kt[p], k0 + h)),
            pl.BlockSpec((1, tk, w), lambda b, h, p, qt, kt: (b, kt[p], v0 + h)),
            pl.BlockSpec((1, nb, w), lambda b, h, p, qt, kt: (b, 0, h)),
            pl.BlockSpec((tk, LANES), lambda b, h, p, qt, kt: (kt[p], 0)),
        ],
        out_specs=pl.BlockSpec((1, t, w), lambda b, h, p, qt, kt: (b, qt[p], h)),
        scratch_shapes=[pltpu.VMEM((hp * t, 2 * MOBA_HEAD_DIM), BF16),
                        pltpu.VMEM((hp * t, LANES), F32),
                        pltpu.VMEM((hp * t, 2 * MOBA_HEAD_DIM), F32)],
    )
    return pl.pallas_call(
        functools.partial(_moba_kernel, topk=min(MOBA_TOPK, nb)),
        grid_spec=grid_spec,
        out_shape=jax.ShapeDtypeStruct((bsz, seq, MOBA_HEADS * MOBA_HEAD_DIM), BF16),
        compiler_params=_params("arbitrary", "arbitrary", "arbitrary"),
        name="moba_attn",
    )(qi_tab, ki_tab, h3, h3, h3, kmean, onehot)


def _mlp_rows(x, w1_ref, w2_ref, g, b, alpha):
    xb = x.astype(BF16)
    y = alpha * x
    for c in range(w1_ref.shape[1] // COL_CHUNK):
        cols = slice(c * COL_CHUNK, (c + 1) * COL_CHUNK)
        hid = jnp.maximum(jnp.dot(xb, w1_ref[:, cols], preferred_element_type=F32), 0.0)
        y = y + jnp.dot((hid * hid).astype(BF16), w2_ref[cols, :], preferred_element_type=F32)
    return _layer_norm_rows(y, g, b)


def _hy_out_mlp_kernel(a_ref, m_ref, wa_ref, wm_ref, x_ref, g1_ref, b1_ref, w1_ref, w2_ref, g2_ref, b2_ref,
                       o_ref, *, alpha):
    y = alpha * x_ref[...]
    y = y + jnp.dot(a_ref[...], wa_ref[...], preferred_element_type=F32)
    y = y + jnp.dot(m_ref[...], wm_ref[...], preferred_element_type=F32)
    mid = _layer_norm_rows(y, g1_ref[...], b1_ref[...])
    o_ref[...] = _mlp_rows(mid, w1_ref, w2_ref, g2_ref[...], b2_ref[...], alpha)


def _hy_out_mlp(a2, m2, w_out, x2, g1, b1, w1, w2, g2, b2, alpha):
    tokens, d = x2.shape
    wa = a2.shape[1]
    wm = m2.shape[1]
    dff = w1.shape[1]
    row = lambda width: pl.BlockSpec((ROW_TILE, width), lambda i: (i, 0))
    vec = _const_spec((1, d))
    return pl.pallas_call(
        functools.partial(_hy_out_mlp_kernel, alpha=alpha),
        grid=(tokens // ROW_TILE,),
        in_specs=[row(wa), row(wm), _const_spec((wa, d)), _const_spec((wm, d)), row(d), vec, vec,
                  _const_spec((d, dff)), _const_spec((dff, d)), vec, vec],
        out_specs=row(d),
        out_shape=jax.ShapeDtypeStruct((tokens, d), F32),
        compiler_params=_params("arbitrary"),
        name="hy_out_mlp",
    )(a2, m2, w_out[:wa], w_out[wa:], x2, g1.reshape(1, d), b1.reshape(1, d), w1, w2,
      g2.reshape(1, d), b2.reshape(1, d))


def _gla_out_mlp_kernel(o_in_ref, r_ref, w_ref, x_ref, g1_ref, b1_ref, w1_ref, w2_ref, g2_ref, b2_ref,
                        o_ref, *, alpha):
    r = r_ref[...].astype(F32)
    gated = o_in_ref[...].astype(F32) * (r * jax.nn.sigmoid(r))
    y = alpha * x_ref[...] + jnp.dot(gated.astype(BF16), w_ref[...], preferred_element_type=F32)
    mid = _layer_norm_rows(y, g1_ref[...], b1_ref[...])
    o_ref[...] = _mlp_rows(mid, w1_ref, w2_ref, g2_ref[...], b2_ref[...], alpha)


def _gla_out_mlp(o2, h2, w_out, x2, g1, b1, w1, w2, g2, b2, alpha):
    tokens, d = x2.shape
    vw = o2.shape[1]
    dff = w1.shape[1]
    r_block = (h2.shape[1] - vw) // vw
    assert r_block * vw + vw == h2.shape[1]
    row = lambda width: pl.BlockSpec((ROW_TILE, width), lambda i: (i, 0))
    vec = _const_spec((1, d))
    return pl.pallas_call(
        functools.partial(_gla_out_mlp_kernel, alpha=alpha),
        grid=(tokens // ROW_TILE,),
        in_specs=[row(vw), pl.BlockSpec((ROW_TILE, vw), lambda i: (i, r_block)),
                  _const_spec((vw, d)), row(d), vec, vec,
                  _const_spec((d, dff)), _const_spec((dff, d)), vec, vec],
        out_specs=row(d),
        out_shape=jax.ShapeDtypeStruct((tokens, d), F32),
        compiler_params=_params("arbitrary"),
        name="gla_out_mlp",
    )(o2, h2, w_out, x2, g1.reshape(1, d), b1.reshape(1, d), w1, w2, g2.reshape(1, d), b2.reshape(1, d))


def _gla_inproj_kernel(x_ref, w_ref, wd_ref, wu_ref, bg_ref, h_ref, g_ref):
    xb = x_ref[...].astype(BF16)
    for c in range(w_ref.shape[1] // COL_CHUNK):
        cols = slice(c * COL_CHUNK, (c + 1) * COL_CHUNK)
        h_ref[:, cols] = jnp.dot(xb, w_ref[:, cols], preferred_element_type=F32).astype(BF16)
    low = jnp.dot(xb, wd_ref[...], preferred_element_type=F32)
    z = jnp.dot(low.astype(BF16), wu_ref[...], preferred_element_type=F32) + bg_ref[...]
    log_sig = jnp.minimum(z, 0.0) - jnp.log1p(jnp.exp(-jnp.abs(z)))
    g_ref[...] = log_sig / GLA_GATE_NORM


def _gla_inproj(x2, w_main, w_down, w_up, b_gate):
    tokens, d = x2.shape
    width = w_main.shape[1]
    kw = w_up.shape[1]
    row = lambda w_: pl.BlockSpec((ROW_TILE, w_), lambda i: (i, 0))
    return pl.pallas_call(
        _gla_inproj_kernel,
        grid=(tokens // ROW_TILE,),
        in_specs=[row(d), _const_spec((d, width)), _const_spec((d, LANES)),
                  _const_spec((LANES, kw)), _const_spec((1, kw))],
        out_specs=[row(width), row(kw)],
        out_shape=[jax.ShapeDtypeStruct((tokens, width), BF16),
                   jax.ShapeDtypeStruct((tokens, kw), F32)],
        compiler_params=_params("arbitrary"),
        name="gla_inproj",
    )(x2, w_main, w_down, w_up, b_gate.reshape(1, kw))


def _gla_level_ref(row_of, dk, half, rows):
    span = 2 * half
    pieces = []
    if span >= 8:
        for p in range(rows // span):
            r = p * span + half - 1
            pieces.append(jnp.broadcast_to(row_of(r), (span, dk)))
    else:
        sub = lax.broadcasted_iota(jnp.int32, (8, dk), 0)
        for base in range(0, rows, 8):
            piece = None
            for p in range(8 // span):
                r = base + p * span + half - 1
                bc = jnp.broadcast_to(row_of(r), (8, dk))
                piece = bc if piece is None else jnp.where(sub >= p * span, bc, piece)
            pieces.append(piece)
    return jnp.concatenate(pieces, axis=0)


def _gla_kernel(q_ref, k_ref, v_ref, g_ref, ng_ref, o_ref, st_ref, b_scr, *, scale):
    L = GLA_CHUNK
    dk, dv, heads = GLA_DK, GLA_DV, GLA_HEADS
    n_chunks = q_ref.shape[1] // L
    levels = [L >> (i + 1) for i in range(L.bit_length() - 1)]

    @pl.when(pl.program_id(1) == 0)
    def _reset():
        st_ref[...] = jnp.zeros(st_ref.shape, F32)

    r_i = lax.broadcasted_iota(jnp.int32, (L, L), 0)
    c_i = lax.broadcasted_iota(jnp.int32, (L, L), 1)
    tri = (c_i <= r_i).astype(BF16)
    level_of = jnp.where(c_i == r_i, -1, -2)
    for idx, half in enumerate(levels):
        span = 2 * half
        member = ((r_i // span) == (c_i // span)) & ((r_i % span) >= half) & ((c_i % span) < half)
        level_of = jnp.where(member, idx, level_of)

    def chunk(ci, carry):
        rows = pl.ds(pl.multiple_of(ci * L, L), L)
        hs = range(heads)
        kcols = [slice(h * dk, (h + 1) * dk) for h in hs]
        vcols = [slice(h * dv, (h + 1) * dv) for h in hs]
        csum = [jnp.dot(tri, jnp.concatenate(_split3(g_ref[0, rows, kcols[h]]), axis=1),
                        preferred_element_type=F32) for h in hs]
        b = [c[:, 0:dk] + c[:, dk:2 * dk] + c[:, 2 * dk:3 * dk] for c in csum]
        for h in hs:
            b_scr[h] = b[h]
        qb = [q_ref[0, rows, kcols[h]] for h in hs]
        kb = [k_ref[0, rows, kcols[h]] for h in hs]
        v = [v_ref[0, rows, vcols[h]] for h in hs]
        qf = [qb[h].astype(F32) * scale for h in hs]
        kf = [kb[h].astype(F32) for h in hs]
        state = [st_ref[h] for h in hs]

        o = [jnp.dot((qf[h] * jnp.exp(b[h])).astype(BF16), state[h].astype(BF16),
                     preferred_element_type=F32) for h in hs]
        att = [jnp.where(level_of == -1, _nt_dot(qb[h], kb[h]) * scale, 0.0) for h in hs]
        for idx, half in enumerate(levels):
            for h in hs:
                ref = _gla_level_ref(lambda r, h=h: b_scr[h, r:r + 1, :], dk, half, L)
                e = jnp.exp(-jnp.abs(b[h] - ref))
                a = _nt_dot((qf[h] * e).astype(BF16), (kf[h] * e).astype(BF16))
                att[h] = jnp.where(level_of == idx, a, att[h])

        for h in hs:
            b_last = b_scr[h, L - 1:L, :]
            kd = (kf[h] * jnp.exp(b_last - b[h])).astype(BF16)
            decay = jnp.transpose(jnp.broadcast_to(jnp.exp(b_last), (dk, dk)))
            decay = jnp.concatenate([decay] * (dv // dk), axis=1)
            st_ref[h] = decay * state[h] + _tn_dot(kd, v[h])

        for h in hs:
            oh = o[h] + jnp.dot(att[h].astype(BF16), v[h], preferred_element_type=F32)
            y = oh * lax.rsqrt(jnp.mean(oh * oh, axis=-1, keepdims=True) + RMS_EPS)
            o_ref[0, rows, vcols[h]] = (y * ng_ref[...]).astype(o_ref.dtype)
        return carry

    lax.fori_loop(0, n_chunks, chunk, 0)


def _gla_recurrence(h3, g3, norm_g):
    bsz, seq, _ = h3.shape
    t = GLA_STEP
    assert seq % t == 0 and t % GLA_CHUNK == 0 and GLA_DV % GLA_DK == 0
    kw = GLA_HEADS * GLA_DK
    vw = GLA_HEADS * GLA_DV
    assert (2 * kw) % vw == 0
    return pl.pallas_call(
        functools.partial(_gla_kernel, scale=GLA_DK ** -0.5),
        grid=(bsz, seq // t),
        in_specs=[
            pl.BlockSpec((1, t, kw), lambda b, s: (b, s, 0)),
            pl.BlockSpec((1, t, kw), lambda b, s: (b, s, 1)),
            pl.BlockSpec((1, t, vw), lambda b, s: (b, s, (2 * kw) // vw)),
            pl.BlockSpec((1, t, kw), lambda b, s: (b, s, 0)),
            pl.BlockSpec((1, GLA_DV), lambda b, s: (0, 0)),
        ],
        out_specs=pl.BlockSpec((1, t, vw), lambda b, s: (b, s, 0)),
        out_shape=jax.ShapeDtypeStruct((bsz, seq, vw), BF16),
        scratch_shapes=[pltpu.VMEM((GLA_HEADS, GLA_DK, GLA_DV), F32),
                        pltpu.VMEM((GLA_HEADS, GLA_CHUNK, GLA_DK), F32)],
        compiler_params=_params("arbitrary", "arbitrary"),
        name="gla_recurrence",
    )(h3, h3, h3, g3, norm_g.reshape(1, GLA_DV))


def kernel(x, hy_w_in, diff_lambda, diff_subln, hy_w_out, gla_w_in, gla_w_gate_up, gla_b_gate,
           gla_norm, gla_w_out, ln_mix_g, ln_mix_b, ffn_w1, ffn_w2, ln_ffn_g, ln_ffn_b):
    bsz, seq, d = x.shape
    depth = ln_mix_g.shape[0]
    alpha = (2 * depth) ** 0.25
    tokens = bsz * seq
    assert tokens % ROW_TILE == 0

    tabs = _rope_tables(seq, DIFF_QK_DIM) + _rope_tables(seq, MOBA_HEAD_DIM)
    gla_main = 2 * GLA_HEADS * GLA_DK + 2 * GLA_HEADS * GLA_DV

    x2 = x.reshape(tokens, d)
    for layer in range(depth):
        w1 = ffn_w1[layer].astype(BF16)
        w2 = ffn_w2[layer].astype(BF16)
        if layer % 2 == 0:
            e = layer // 2
            lambda_init = 0.8 - 0.6 * math.exp(-0.3 * layer)
            h2, kmean = _hy_inproj(x2, hy_w_in[e].astype(BF16), tabs, seq)
            h3 = h2.reshape(bsz, seq, h2.shape[1])
            kmean = kmean.reshape(bsz, seq // MOBA_BLOCK, kmean.shape[-1])
            a = _diff_attention(h3, diff_lambda[e], diff_subln[e], lambda_init)
            m = _moba_attention(h3, kmean)
            x2 = _hy_out_mlp(a.reshape(tokens, -1), m.reshape(tokens, -1), hy_w_out[e].astype(BF16), x2,
                             ln_mix_g[layer], ln_mix_b[layer], w1, w2, ln_ffn_g[layer], ln_ffn_b[layer], alpha)
        else:
            o = layer // 2
            w_in = gla_w_in[o]
            w_down = jnp.pad(w_in[:, gla_main:], ((0, 0), (0, LANES - GLA_GATE_RANK))).astype(BF16)
            w_up = jnp.pad(gla_w_gate_up[o], ((0, LANES - GLA_GATE_RANK), (0, 0))).astype(BF16)
            h2, g2 = _gla_inproj(x2, w_in[:, :gla_main].astype(BF16), w_down, w_up, gla_b_gate[o])
            h3 = h2.reshape(bsz, seq, h2.shape[1])
            g3 = g2.reshape(bsz, seq, g2.shape[1])
            og = _gla_recurrence(h3, g3, gla_norm[o])
            x2 = _gla_out_mlp(og.reshape(tokens, -1), h2, gla_w_out[o].astype(BF16), x2,
                              ln_mix_g[layer], ln_mix_b[layer], w1, w2, ln_ffn_g[layer], ln_ffn_b[layer], alpha)
    return x2.reshape(bsz, seq, d)
```

```python
import functools
import math

import jax
import jax.numpy as jnp
import numpy as np
from jax import lax
from jax.experimental import pallas as pl
from jax.experimental.pallas import tpu as pltpu

F32 = jnp.float32
BF16 = jnp.bfloat16

DIFF_HEADS = 4
DIFF_QK_DIM = 64
DIFF_V_DIM = 128
MOBA_HEADS = 4
MOBA_HEAD_DIM = 128
MOBA_BLOCK = 256
MOBA_TOPK = 3
GLA_HEADS = 4
GLA_DK = 128
GLA_DV = 256
GLA_GATE_RANK = 16
GLA_GATE_NORM = 16.0
ROPE_THETA = 10000.0
LN_EPS = 1e-5
RMS_EPS = 1e-5

LANES = 128
VMEM_LIMIT_BYTES = 56 * 1024 * 1024

ROW_TILE = 512
COL_CHUNK = 512
OUT_MLP_TILE = 1024
OUT_MLP_SPLIT = 2
ATTN_TILE = 512
ATTN_KEY_TILE = 1024
ATTN_ROWS = 128
ATTN_HEADS_PER_STEP = 4
LOG2E = math.log2(math.e)
GLA_CHUNK = 64
GLA_GROUP = 4
GLA_STEP = 512
NEG_BIG = -1e30


def _nt_dot(a, b, precision=None):
    return lax.dot_general(a, b, (((1,), (1,)), ((), ())),
                           preferred_element_type=F32, precision=precision)


def _tn_dot(a, b, precision=None):
    return lax.dot_general(a, b, (((0,), (0,)), ((), ())),
                           preferred_element_type=F32, precision=precision)


def _split3(x):
    x1 = x.astype(BF16)
    r1 = x - x1.astype(F32)
    x2 = r1.astype(BF16)
    x3 = (r1 - x2.astype(F32)).astype(BF16)
    return x1, x2, x3


def _const_spec(shape):
    nd = len(shape)
    return pl.BlockSpec(shape, lambda *_: (0,) * nd, pipeline_mode=pl.Buffered(1))


def _params(*sem):
    return pltpu.CompilerParams(dimension_semantics=sem, vmem_limit_bytes=VMEM_LIMIT_BYTES)


def _layer_norm_rows(y, g, b):
    mu = jnp.mean(y, axis=-1, keepdims=True)
    d = y - mu
    var = jnp.mean(d * d, axis=-1, keepdims=True)
    return d * lax.rsqrt(var + LN_EPS) * g + b


def _rope_tables(seq, group):
    half = group // 2
    inv = 1.0 / (ROPE_THETA ** (jnp.arange(0, group, 2, dtype=F32) / group))
    ang = jnp.arange(seq, dtype=F32)[:, None] * inv[None, :]
    cos, sin = jnp.cos(ang), jnp.sin(ang)
    cos_g = jnp.concatenate([cos, cos], axis=-1)
    sin_g = jnp.concatenate([-sin, sin], axis=-1)
    reps = LANES // group
    return jnp.tile(cos_g, (1, reps)), jnp.tile(sin_g, (1, reps))


def _rope(acc, cos, sin, group):
    width = acc.shape[1]
    half = group // 2
    lane = lax.broadcasted_iota(jnp.int32, acc.shape, 1)
    upper = pltpu.roll(acc, width - half, axis=1)
    lower = pltpu.roll(acc, half, axis=1)
    partner = jnp.where((lane % group) < half, upper, lower)
    reps = width // LANES
    c = jnp.concatenate([cos] * reps, axis=1)
    s = jnp.concatenate([sin] * reps, axis=1)
    return acc * c + partner * s


def _hy_inproj_kernel(x_ref, w_ref, cd_ref, sd_ref, cm_ref, sm_ref, h_ref, km_ref):
    xb = x_ref[...].astype(BF16)
    rows = xb.shape[0]
    n_chunks = w_ref.shape[1] // COL_CHUNK
    for c in range(n_chunks):
        cols = slice(c * COL_CHUNK, (c + 1) * COL_CHUNK)
        acc = jnp.dot(xb, w_ref[:, cols], preferred_element_type=F32)
        if c == 0:
            acc = _rope(acc, cd_ref[...], sd_ref[...], DIFF_QK_DIM) * (DIFF_QK_DIM ** -0.5 * LOG2E)
        elif c == 1:
            acc = _rope(acc, cd_ref[...], sd_ref[...], DIFF_QK_DIM)
        elif c == 3:
            acc = _rope(acc, cm_ref[...], sm_ref[...], MOBA_HEAD_DIM) * (MOBA_HEAD_DIM ** -0.5 * LOG2E)
        elif c == 4:
            acc = _rope(acc, cm_ref[...], sm_ref[...], MOBA_HEAD_DIM)
        if c == 4:
            blocks = rows // MOBA_BLOCK
            km_ref[0] = jnp.mean(acc.reshape(blocks, MOBA_BLOCK, COL_CHUNK), axis=1)
        h_ref[:, cols] = acc.astype(BF16)


def _hy_inproj(x2, w, tabs, seq):
    tokens, d = x2.shape
    width = w.shape[1]
    assert width == 6 * COL_CHUNK and seq % ROW_TILE == 0 and ROW_TILE % MOBA_BLOCK == 0
    steps = tokens // ROW_TILE
    per_seq = seq // ROW_TILE
    blocks = ROW_TILE // MOBA_BLOCK
    tab_spec = pl.BlockSpec((ROW_TILE, LANES), lambda i: (i % per_seq, 0))
    return pl.pallas_call(
        _hy_inproj_kernel,
        grid=(steps,),
        in_specs=[pl.BlockSpec((ROW_TILE, d), lambda i: (i, 0)),
                  _const_spec((d, width)),
                  tab_spec, tab_spec, tab_spec, tab_spec],
        out_specs=[pl.BlockSpec((ROW_TILE, width), lambda i: (i, 0)),
                   pl.BlockSpec((1, blocks, COL_CHUNK), lambda i: (i, 0, 0))],
        out_shape=[jax.ShapeDtypeStruct((tokens, width), BF16),
                   jax.ShapeDtypeStruct((steps, blocks, COL_CHUNK), F32)],
        compiler_params=_params("arbitrary"),
        name="hy_inproj",
    )(x2, w, *tabs)


def _causal_pairs(n_q, ratio):
    qi = np.array([q for q in range(n_q) for _ in range(q // ratio + 1)], np.int32)
    ki = np.array([k for q in range(n_q) for k in range(q // ratio + 1)], np.int32)
    return jnp.asarray(qi), jnp.asarray(ki)


def _for_each_tile_kind(qi, ki, tq, tk, step, finish):
    ratio = tk // tq

    @pl.when(ki < qi // ratio)
    def _full():
        step(None)

    for part in range(ratio):
        @pl.when((ki == qi // ratio) & (qi % ratio == part))
        def _diag(part=part):
            step(part * tq)
            finish()


def _with_ones(v):
    return jnp.concatenate([v, jnp.ones(v.shape, v.dtype)], axis=1)


def _flash_update(s, v_ext, m_ref, acc_ref, rows):
    reps = s.shape[1] // LANES
    m_prev = m_ref[rows, :]
    m_new = jnp.maximum(m_prev, jnp.max(s, axis=-1, keepdims=True))
    alpha = jnp.exp2(m_prev - m_new)
    p = jnp.exp2(s - jnp.concatenate([m_new] * reps, axis=1))
    pv = jnp.dot(p.astype(BF16), v_ext, preferred_element_type=F32)
    acc_ref[rows, :] = jnp.concatenate([alpha, alpha], axis=1) * acc_ref[rows, :] + pv
    m_ref[rows, :] = m_new


def _attend_tile(s_all, v, m_ref, acc_ref, row0, tq, diag_offset):
    rb = ATTN_ROWS
    v_ext = _with_ones(v)
    for r in range(s_all.shape[0] // rb):
        rows = slice(row0 + r * rb, row0 + (r + 1) * rb)
        s = s_all[r * rb:(r + 1) * rb]
        if diag_offset is not None:
            first = diag_offset + (r * rb) % tq
            keys = first + rb
            row = lax.broadcasted_iota(jnp.int32, (rb, keys), 0) + first
            col = lax.broadcasted_iota(jnp.int32, (rb, keys), 1)
            s = jnp.where(col <= row, s[:, 0:keys], NEG_BIG)
            _flash_update(s, v_ext[0:keys], m_ref, acc_ref, rows)
        else:
            _flash_update(s, v_ext, m_ref, acc_ref, rows)


def _diff_attn_kernel(qi_tab, ki_tab, q_ref, k_ref, v_ref, lam_ref, sub_ref, o_ref,
                      qs_scr, m_scr, acc_scr, *, lambda_init):
    p_id = pl.program_id(2)
    qi = qi_tab[p_id]
    ki = ki_tab[p_id]
    tq = q_ref.shape[1]
    heads = q_ref.shape[2] // LANES
    dv = DIFF_V_DIM

    @pl.when(ki == 0)
    def _init():
        for hh in range(heads):
            q = q_ref[0, :, hh * LANES:(hh + 1) * LANES]
            lane = lax.broadcasted_iota(jnp.int32, q.shape, 1)
            zero = jnp.zeros_like(q)
            base = hh * 2 * tq
            qs_scr[base:base + tq, :] = jnp.where(lane < DIFF_QK_DIM, q, zero)
            qs_scr[base + tq:base + 2 * tq, :] = jnp.where(lane >= DIFF_QK_DIM, q, zero)
        m_scr[...] = jnp.full(m_scr.shape, NEG_BIG, F32)
        acc_scr[...] = jnp.zeros(acc_scr.shape, F32)

    def step(diag_offset):
        scores = [_nt_dot(qs_scr[hh * 2 * tq:(hh + 1) * 2 * tq, :], k_ref[0, :, hh * LANES:(hh + 1) * LANES])
                  for hh in range(heads)]
        for hh in range(heads):
            _attend_tile(scores[hh], v_ref[0, :, hh * dv:(hh + 1) * dv], m_scr, acc_scr, hh * 2 * tq, tq,
                         diag_offset)

    def finish():
        lam = lam_ref[...].astype(F32)
        lam_full = (jnp.exp(jnp.sum(lam[0:1] * lam[1:2], axis=-1, keepdims=True))
                    - jnp.exp(jnp.sum(lam[2:3] * lam[3:4], axis=-1, keepdims=True))
                    + lambda_init)
        for hh in range(heads):
            base = hh * 2 * tq
            o1 = acc_scr[base:base + tq, 0:dv] / acc_scr[base:base + tq, dv:]
            o2 = acc_scr[base + tq:base + 2 * tq, 0:dv] / acc_scr[base + tq:base + 2 * tq, dv:]
            o = o1 - lam_full * o2
            y = o * lax.rsqrt(jnp.mean(o * o, axis=-1, keepdims=True) + RMS_EPS)
            o_ref[0, :, hh * dv:(hh + 1) * dv] = (y * sub_ref[...] * (1.0 - lambda_init)).astype(o_ref.dtype)

    _for_each_tile_kind(qi, ki, tq, k_ref.shape[1], step, finish)


def _diff_attention(h3, lam, subln, lambda_init):
    bsz, seq, _ = h3.shape
    t = ATTN_TILE
    tk = ATTN_KEY_TILE
    assert seq % tk == 0 and tk % t == 0
    qi_tab, ki_tab = _causal_pairs(seq // t, tk // t)
    hp = ATTN_HEADS_PER_STEP
    assert DIFF_HEADS % hp == 0 and 2 * DIFF_QK_DIM == LANES and DIFF_V_DIM == LANES
    groups = DIFF_HEADS // hp
    w = hp * LANES
    grid_spec = pltpu.PrefetchScalarGridSpec(
        num_scalar_prefetch=2,
        grid=(bsz, groups, int(qi_tab.shape[0])),
        in_specs=[
            pl.BlockSpec((1, t, w), lambda b, h, p, qt, kt: (b, qt[p], h)),
            pl.BlockSpec((1, tk, w), lambda b, h, p, qt, kt: (b, kt[p], groups + h)),
            pl.BlockSpec((1, tk, w), lambda b, h, p, qt, kt: (b, kt[p], 2 * groups + h)),
            pl.BlockSpec((4, DIFF_QK_DIM), lambda b, h, p, qt, kt: (0, 0)),
            pl.BlockSpec((1, DIFF_V_DIM), lambda b, h, p, qt, kt: (0, 0)),
        ],
        out_specs=pl.BlockSpec((1, t, w), lambda b, h, p, qt, kt: (b, qt[p], h)),
        scratch_shapes=[pltpu.VMEM((hp * 2 * t, LANES), BF16),
                        pltpu.VMEM((hp * 2 * t, LANES), F32),
                        pltpu.VMEM((hp * 2 * t, 2 * DIFF_V_DIM), F32)],
    )
    return pl.pallas_call(
        functools.partial(_diff_attn_kernel, lambda_init=lambda_init),
        grid_spec=grid_spec,
        out_shape=jax.ShapeDtypeStruct((bsz, seq, DIFF_HEADS * DIFF_V_DIM), BF16),
        compiler_params=_params("arbitrary", "arbitrary", "arbitrary"),
        name="diff_attn",
    )(qi_tab, ki_tab, h3, h3, h3, lam, subln.reshape(1, DIFF_V_DIM))


def _moba_kernel(qi_tab, ki_tab, q_ref, k_ref, v_ref, km_ref, oh_ref, o_ref,
                 qa_scr, m_scr, acc_scr, *, topk):
    p_id = pl.program_id(2)
    qi = qi_tab[p_id]
    ki = ki_tab[p_id]
    tq = q_ref.shape[1]
    d = MOBA_HEAD_DIM
    heads = q_ref.shape[2] // d

    @pl.when(ki == 0)
    def _init():
        nb = km_ref.shape[1]
        blk = lax.broadcasted_iota(jnp.int32, (nb, tq), 0)
        pos = lax.broadcasted_iota(jnp.int32, (nb, tq), 1)
        own = (qi * tq + pos) // MOBA_BLOCK
        for hh in range(heads):
            q = q_ref[0, :, hh * d:(hh + 1) * d]
            km3 = jnp.concatenate(_split3(km_ref[0, :, hh * d:(hh + 1) * d]), axis=0)
            g3 = _nt_dot(km3, q)
            gate = g3[0:nb] + g3[nb:2 * nb] + g3[2 * nb:3 * nb]
            g = jnp.where(blk < own, gate, -jnp.inf)
            sel = blk == own
            for _ in range(topk):
                mx = jnp.max(g, axis=0, keepdims=True)
                idx = jnp.min(jnp.where(g == mx, blk, nb), axis=0, keepdims=True)
                sel = sel | ((blk == idx) & (mx > -jnp.inf))
                g = jnp.where(blk == idx, -jnp.inf, g)
            bias_t = jnp.concatenate([jnp.where(sel, 0.0, NEG_BIG),
                                      jnp.full((LANES - nb, tq), NEG_BIG, F32)], axis=0)
            qa_scr[hh * tq:(hh + 1) * tq, 0:d] = q
            qa_scr[hh * tq:(hh + 1) * tq, d:2 * d] = jnp.transpose(bias_t).astype(BF16)
        m_scr[...] = jnp.full(m_scr.shape, NEG_BIG, F32)
        acc_scr[...] = jnp.zeros(acc_scr.shape, F32)

    def step(diag_offset):
        onehot = oh_ref[...]
        scores = [_nt_dot(qa_scr[hh * tq:(hh + 1) * tq, :],
                          jnp.concatenate([k_ref[0, :, hh * d:(hh + 1) * d], onehot], axis=1))
                  for hh in range(heads)]
        for hh in range(heads):
            _attend_tile(scores[hh], v_ref[0, :, hh * d:(hh + 1) * d], m_scr, acc_scr, hh * tq, tq, diag_offset)

    def finish():
        for hh in range(heads):
            rows = slice(hh * tq, (hh + 1) * tq)
            o_ref[0, :, hh * d:(hh + 1) * d] = (acc_scr[rows, 0:d] / acc_scr[rows, d:2 * d]).astype(o_ref.dtype)

    _for_each_tile_kind(qi, ki, tq, k_ref.shape[1], step, finish)


def _moba_attention(h3, kmean):
    bsz, seq, _ = h3.shape
    t = ATTN_TILE
    tk = ATTN_KEY_TILE
    nb = seq // MOBA_BLOCK
    assert seq % tk == 0 and tk % t == 0 and t % MOBA_BLOCK == 0 and nb <= LANES and MOBA_HEAD_DIM == LANES
    qi_tab, ki_tab = _causal_pairs(seq // t, tk // t)
    hp = ATTN_HEADS_PER_STEP
    assert MOBA_HEADS % hp == 0
    groups = MOBA_HEADS // hp
    w = hp * MOBA_HEAD_DIM
    q0 = (3 * DIFF_HEADS * DIFF_V_DIM) // w
    k0 = q0 + groups
    v0 = k0 + groups
    block_of_key = jnp.arange(seq, dtype=jnp.int32)[:, None] // MOBA_BLOCK
    onehot = (block_of_key == jnp.arange(LANES, dtype=jnp.int32)[None, :]).astype(BF16)
    grid_spec = pltpu.PrefetchScalarGridSpec(
        num_scalar_prefetch=2,
        grid=(bsz, groups, int(qi_tab.shape[0])),
        in_specs=[
            pl.BlockSpec((1, t, w), lambda b, h, p, qt, kt: (b, qt[p], q0 + h)),
            pl.BlockSpec((1, tk, w), lambda b, h, p, qt, kt: (b, kt[p], k0 + h)),
            pl.BlockSpec((1, tk, w), lambda b, h, p, qt, kt: (b, kt[p], v0 + h)),
            pl.BlockSpec((1, nb, w), lambda b, h, p, qt, kt: (b, 0, h)),
            pl.BlockSpec((tk, LANES), lambda b, h, p, qt, kt: (kt[p], 0)),
        ],
        out_specs=pl.BlockSpec((1, t, w), lambda b, h, p, qt, kt: (b, qt[p], h)),
        scratch_shapes=[pltpu.VMEM((hp * t, 2 * MOBA_HEAD_DIM), BF16),
                        pltpu.VMEM((hp * t, LANES), F32),
                        pltpu.VMEM((hp * t, 2 * MOBA_HEAD_DIM), F32)],
    )
    return pl.pallas_call(
        functools.partial(_moba_kernel, topk=min(MOBA_TOPK, nb)),
        grid_spec=grid_spec,
        out_shape=jax.ShapeDtypeStruct((bsz, seq, MOBA_HEADS * MOBA_HEAD_DIM), BF16),
        compiler_params=_params("arbitrary", "arbitrary", "arbitrary"),
        name="moba_attn",
    )(qi_tab, ki_tab, h3, h3, h3, kmean, onehot)


def _mlp_rows(x, w1_ref, w2_ref, g, b, alpha):
    xb = x.astype(BF16)
    y = alpha * x
    for c in range(w1_ref.shape[1] // COL_CHUNK):
        cols = slice(c * COL_CHUNK, (c + 1) * COL_CHUNK)
        hid = jnp.maximum(jnp.dot(xb, w1_ref[:, cols], preferred_element_type=F32), 0.0)
        y = y + jnp.dot((hid * hid).astype(BF16), w2_ref[cols, :], preferred_element_type=F32)
    return _layer_norm_rows(y, g, b)


def _norm_then_mlp(y_halves, g1, b1, w1_ref, w2_ref, g2, b2, o_ref, alpha):
    mids = [_layer_norm_rows(y, g1, b1) for y in y_halves]
    row0 = 0
    for mid in mids:
        o_ref[row0:row0 + mid.shape[0], :] = _mlp_rows(mid, w1_ref, w2_ref, g2, b2, alpha)
        row0 += mid.shape[0]


def _row_halves(rows):
    half = rows // OUT_MLP_SPLIT
    return [slice(i * half, (i + 1) * half) for i in range(OUT_MLP_SPLIT)]


def _hy_out_mlp_kernel(a_ref, m_ref, wa_ref, wm_ref, x_ref, g1_ref, b1_ref, w1_ref, w2_ref, g2_ref, b2_ref,
                       o_ref, *, alpha):
    ys = []
    for rows in _row_halves(x_ref.shape[0]):
        y = alpha * x_ref[rows, :]
        y = y + jnp.dot(a_ref[rows, :], wa_ref[...], preferred_element_type=F32)
        ys.append(y + jnp.dot(m_ref[rows, :], wm_ref[...], preferred_element_type=F32))
    _norm_then_mlp(ys, g1_ref[...], b1_ref[...], w1_ref, w2_ref, g2_ref[...], b2_ref[...], o_ref, alpha)


def _hy_out_mlp(a2, m2, w_out, x2, g1, b1, w1, w2, g2, b2, alpha):
    tokens, d = x2.shape
    wa = a2.shape[1]
    wm = m2.shape[1]
    dff = w1.shape[1]
    t = OUT_MLP_TILE
    assert tokens % t == 0
    row = lambda width: pl.BlockSpec((t, width), lambda i: (i, 0))
    vec = _const_spec((1, d))
    return pl.pallas_call(
        functools.partial(_hy_out_mlp_kernel, alpha=alpha),
        grid=(tokens // t,),
        in_specs=[row(wa), row(wm), _const_spec((wa, d)), _const_spec((wm, d)), row(d), vec, vec,
                  _const_spec((d, dff)), _const_spec((dff, d)), vec, vec],
        out_specs=row(d),
        out_shape=jax.ShapeDtypeStruct((tokens, d), F32),
        compiler_params=_params("arbitrary"),
        name="hy_out_mlp",
    )(a2, m2, w_out[:wa], w_out[wa:], x2, g1.reshape(1, d), b1.reshape(1, d), w1, w2,
      g2.reshape(1, d), b2.reshape(1, d))


def _gla_out_mlp_kernel(o_in_ref, r_ref, w_ref, x_ref, g1_ref, b1_ref, w1_ref, w2_ref, g2_ref, b2_ref,
                        o_ref, *, alpha):
    ys = []
    for rows in _row_halves(x_ref.shape[0]):
        r = r_ref[rows, :].astype(F32)
        gated = o_in_ref[rows, :].astype(F32) * (r * jax.nn.sigmoid(r))
        ys.append(alpha * x_ref[rows, :] + jnp.dot(gated.astype(BF16), w_ref[...], preferred_element_type=F32))
    _norm_then_mlp(ys, g1_ref[...], b1_ref[...], w1_ref, w2_ref, g2_ref[...], b2_ref[...], o_ref, alpha)


def _gla_out_mlp(o2, h2, w_out, x2, g1, b1, w1, w2, g2, b2, alpha):
    tokens, d = x2.shape
    vw = o2.shape[1]
    dff = w1.shape[1]
    t = OUT_MLP_TILE
    r_block = (h2.shape[1] - vw) // vw
    assert r_block * vw + vw == h2.shape[1] and tokens % t == 0
    row = lambda width: pl.BlockSpec((t, width), lambda i: (i, 0))
    vec = _const_spec((1, d))
    return pl.pallas_call(
        functools.partial(_gla_out_mlp_kernel, alpha=alpha),
        grid=(tokens // t,),
        in_specs=[row(vw), pl.BlockSpec((t, vw), lambda i: (i, r_block)),
                  _const_spec((vw, d)), row(d), vec, vec,
                  _const_spec((d, dff)), _const_spec((dff, d)), vec, vec],
        out_specs=row(d),
        out_shape=jax.ShapeDtypeStruct((tokens, d), F32),
        compiler_params=_params("arbitrary"),
        name="gla_out_mlp",
    )(o2, h2, w_out, x2, g1.reshape(1, d), b1.reshape(1, d), w1, w2, g2.reshape(1, d), b2.reshape(1, d))


def _gla_inproj_kernel(x_ref, w_ref, wd_ref, wu_ref, bg_ref, h_ref, g_ref):
    xb = x_ref[...].astype(BF16)
    low = jnp.dot(xb, wd_ref[...], preferred_element_type=F32)
    z = jnp.dot(low.astype(BF16), wu_ref[...], preferred_element_type=F32) + bg_ref[...]
    log_sig = jnp.minimum(z, 0.0) - jnp.log1p(jnp.exp(-jnp.abs(z)))
    g_ref[...] = log_sig / GLA_GATE_NORM
    for c in range(w_ref.shape[1] // COL_CHUNK):
        cols = slice(c * COL_CHUNK, (c + 1) * COL_CHUNK)
        h_ref[:, cols] = jnp.dot(xb, w_ref[:, cols], preferred_element_type=F32).astype(BF16)


def _gla_inproj(x2, w_main, w_down, w_up, b_gate):
    tokens, d = x2.shape
    width = w_main.shape[1]
    kw = w_up.shape[1]
    row = lambda w_: pl.BlockSpec((ROW_TILE, w_), lambda i: (i, 0))
    return pl.pallas_call(
        _gla_inproj_kernel,
        grid=(tokens // ROW_TILE,),
        in_specs=[row(d), _const_spec((d, width)), _const_spec((d, LANES)),
                  _const_spec((LANES, kw)), _const_spec((1, kw))],
        out_specs=[row(width), row(kw)],
        out_shape=[jax.ShapeDtypeStruct((tokens, width), BF16),
                   jax.ShapeDtypeStruct((tokens, kw), F32)],
        compiler_params=_params("arbitrary"),
        name="gla_inproj",
    )(x2, w_main, w_down, w_up, b_gate.reshape(1, kw))


def _gla_level_ref(row_of, dk, half, rows):
    span = 2 * half
    pieces = []
    if span >= 8:
        for p in range(rows // span):
            r = p * span + half - 1
            pieces.append(jnp.broadcast_to(row_of(r), (span, dk)))
    else:
        sub = lax.broadcasted_iota(jnp.int32, (8, dk), 0)
        for base in range(0, rows, 8):
            piece = None
            for p in range(8 // span):
                r = base + p * span + half - 1
                bc = jnp.broadcast_to(row_of(r), (8, dk))
                piece = bc if piece is None else jnp.where(sub >= p * span, bc, piece)
            pieces.append(piece)
    return jnp.concatenate(pieces, axis=0)


def _gla_kernel(q_ref, k_ref, v_ref, g_ref, ng_ref, o_ref, st_ref, b_scr, *, scale):
    L = GLA_CHUNK
    dk, dv, heads = GLA_DK, GLA_DV, GLA_HEADS
    n_chunks = q_ref.shape[1] // L
    levels = [L >> (i + 1) for i in range(L.bit_length() - 1)]

    @pl.when(pl.program_id(1) == 0)
    def _reset():
        st_ref[...] = jnp.zeros(st_ref.shape, F32)

    r_i = lax.broadcasted_iota(jnp.int32, (L, L), 0)
    c_i = lax.broadcasted_iota(jnp.int32, (L, L), 1)
    tri = (c_i <= r_i).astype(BF16)
    level_of = jnp.where(c_i == r_i, -1, -2)
    for idx, half in enumerate(levels):
        span = 2 * half
        member = ((r_i // span) == (c_i // span)) & ((r_i % span) >= half) & ((c_i % span) < half)
        level_of = jnp.where(member, idx, level_of)

    def chunk_group(gi, carry):
        hs = range(heads)
        cs = range(GLA_GROUP)
        kcols = [slice(h * dk, (h + 1) * dk) for h in hs]
        vcols = [slice(h * dv, (h + 1) * dv) for h in hs]
        rows = [pl.ds(pl.multiple_of((gi * GLA_GROUP + c) * L, L), L) for c in cs]
        b = {}
        for c in cs:
            for h in hs:
                cs3 = jnp.dot(tri, jnp.concatenate(_split3(g_ref[0, rows[c], kcols[h]]), axis=1),
                              preferred_element_type=F32)
                b[c, h] = cs3[:, 0:dk] + cs3[:, dk:2 * dk] + cs3[:, 2 * dk:3 * dk]
                b_scr[c, h] = b[c, h]
        qb = {(c, h): q_ref[0, rows[c], kcols[h]] for c in cs for h in hs}
        kb = {(c, h): k_ref[0, rows[c], kcols[h]] for c in cs for h in hs}
        v = {(c, h): v_ref[0, rows[c], vcols[h]] for c in cs for h in hs}
        qf = {key: qb[key].astype(F32) * scale for key in qb}
        kf = {key: kb[key].astype(F32) for key in kb}

        att = {key: jnp.where(level_of == -1, _nt_dot(qb[key], kb[key]) * scale, 0.0) for key in qb}
        for idx, half in enumerate(levels):
            for c in cs:
                for h in hs:
                    ref = _gla_level_ref(lambda r, c=c, h=h: b_scr[c, h, r:r + 1, :], dk, half, L)
                    e = jnp.exp(-jnp.abs(b[c, h] - ref))
                    a = _nt_dot((qf[c, h] * e).astype(BF16), (kf[c, h] * e).astype(BF16))
                    att[c, h] = jnp.where(level_of == idx, a, att[c, h])

        state = [st_ref[h] for h in hs]
        o = {}
        for c in cs:
            for h in hs:
                o[c, h] = jnp.dot((qf[c, h] * jnp.exp(b[c, h])).astype(BF16), state[h].astype(BF16),
                                  preferred_element_type=F32)
            for h in hs:
                b_last = b_scr[c, h, L - 1:L, :]
                kd = (kf[c, h] * jnp.exp(b_last - b[c, h])).astype(BF16)
                decay = jnp.transpose(jnp.broadcast_to(jnp.exp(b_last), (dk, dk)))
                decay = jnp.concatenate([decay] * (dv // dk), axis=1)
                state[h] = decay * state[h] + _tn_dot(kd, v[c, h])
        for h in hs:
            st_ref[h] = state[h]

        for c in cs:
            for h in hs:
                oh = o[c, h] + jnp.dot(att[c, h].astype(BF16), v[c, h], preferred_element_type=F32)
                y = oh * lax.rsqrt(jnp.mean(oh * oh, axis=-1, keepdims=True) + RMS_EPS)
                o_ref[0, rows[c], vcols[h]] = (y * ng_ref[...]).astype(o_ref.dtype)
        return carry

    lax.fori_loop(0, n_chunks // GLA_GROUP, chunk_group, 0)


def _gla_recurrence(h3, g3, norm_g):
    bsz, seq, _ = h3.shape
    t = GLA_STEP
    assert seq % t == 0 and t % (GLA_CHUNK * GLA_GROUP) == 0 and GLA_DV % GLA_DK == 0
    kw = GLA_HEADS * GLA_DK
    vw = GLA_HEADS * GLA_DV
    assert (2 * kw) % vw == 0
    return pl.pallas_call(
        functools.partial(_gla_kernel, scale=GLA_DK ** -0.5),
        grid=(bsz, seq // t),
        in_specs=[
            pl.BlockSpec((1, t, kw), lambda b, s: (b, s, 0)),
            pl.BlockSpec((1, t, kw), lambda b, s: (b, s, 1)),
            pl.BlockSpec((1, t, vw), lambda b, s: (b, s, (2 * kw) // vw)),
            pl.BlockSpec((1, t, kw), lambda b, s: (b, s, 0)),
            pl.BlockSpec((1, GLA_DV), lambda b, s: (0, 0)),
        ],
        out_specs=pl.BlockSpec((1, t, vw), lambda b, s: (b, s, 0)),
        out_shape=jax.ShapeDtypeStruct((bsz, seq, vw), BF16),
        scratch_shapes=[pltpu.VMEM((GLA_HEADS, GLA_DK, GLA_DV), F32),
                        pltpu.VMEM((GLA_GROUP, GLA_HEADS, GLA_CHUNK, GLA_DK), F32)],
        compiler_params=_params("arbitrary", "arbitrary"),
        name="gla_recurrence",
    )(h3, h3, h3, g3, norm_g.reshape(1, GLA_DV))


def kernel(x, hy_w_in, diff_lambda, diff_subln, hy_w_out, gla_w_in, gla_w_gate_up, gla_b_gate,
           gla_norm, gla_w_out, ln_mix_g, ln_mix_b, ffn_w1, ffn_w2, ln_ffn_g, ln_ffn_b):
    bsz, seq, d = x.shape
    depth = ln_mix_g.shape[0]
    alpha = (2 * depth) ** 0.25
    tokens = bsz * seq
    assert tokens % ROW_TILE == 0

    tabs = _rope_tables(seq, DIFF_QK_DIM) + _rope_tables(seq, MOBA_HEAD_DIM)
    gla_main = 2 * GLA_HEADS * GLA_DK + 2 * GLA_HEADS * GLA_DV

    x2 = x.reshape(tokens, d)
    for layer in range(depth):
        w1 = ffn_w1[layer].astype(BF16)
        w2 = ffn_w2[layer].astype(BF16)
        if layer % 2 == 0:
            e = layer // 2
            lambda_init = 0.8 - 0.6 * math.exp(-0.3 * layer)
            h2, kmean = _hy_inproj(x2, hy_w_in[e].astype(BF16), tabs, seq)
            h3 = h2.reshape(bsz, seq, h2.shape[1])
            kmean = kmean.reshape(bsz, seq // MOBA_BLOCK, kmean.shape[-1])
            a = _diff_attention(h3, diff_lambda[e], diff_subln[e], lambda_init)
            m = _moba_attention(h3, kmean)
            x2 = _hy_out_mlp(a.reshape(tokens, -1), m.reshape(tokens, -1), hy_w_out[e].astype(BF16), x2,
                             ln_mix_g[layer], ln_mix_b[layer], w1, w2, ln_ffn_g[layer], ln_ffn_b[layer], alpha)
        else:
            o = layer // 2
            w_in = gla_w_in[o]
            w_down = jnp.pad(w_in[:, gla_main:], ((0, 0), (0, LANES - GLA_GATE_RANK))).astype(BF16)
            w_up = jnp.pad(gla_w_gate_up[o], ((0, LANES - GLA_GATE_RANK), (0, 0))).astype(BF16)
            h2, g2 = _gla_inproj(x2, w_in[:, :gla_main].astype(BF16), w_down, w_up, gla_b_gate[o])
            h3 = h2.reshape(bsz, seq, h2.shape[1])
            g3 = g2.reshape(bsz, seq, g2.shape[1])
            og = _gla_recurrence(h3, g3, gla_norm[o])
            x2 = _gla_out_mlp(og.reshape(tokens, -1), h2, gla_w_out[o].astype(BF16), x2,
                              ln_mix_g[layer], ln_mix_b[layer], w1, w2, ln_ffn_g[layer], ln_ffn_b[layer], alpha)
    return x2.reshape(bsz, seq, d)
```

```python
import functools
import math

import jax
import jax.numpy as jnp
import numpy as np
from jax import lax
from jax.experimental import pallas as pl
from jax.experimental.pallas import tpu as pltpu

F32 = jnp.float32
BF16 = jnp.bfloat16

DIFF_HEADS = 4
DIFF_QK_DIM = 64
DIFF_V_DIM = 128
MOBA_HEADS = 4
MOBA_HEAD_DIM = 128
MOBA_BLOCK = 256
MOBA_TOPK = 3
GLA_HEADS = 4
GLA_DK = 128
GLA_DV = 256
GLA_GATE_RANK = 16
GLA_GATE_NORM = 16.0
ROPE_THETA = 10000.0
LN_EPS = 1e-5
RMS_EPS = 1e-5

LANES = 128
SUBLANES = 8
VMEM_LIMIT_BYTES = 56 * 1024 * 1024

ROW_TILE = 512
COL_CHUNK = 512
OUT_MLP_TILE = 1024
OUT_MLP_SPLIT = 2
ATTN_TILE = 512
DIFF_KEY_TILE = 1024
MOBA_KEY_TILE = 2048
ATTN_ROWS = 128
ATTN_HEADS_PER_STEP = 4
LOG2E = math.log2(math.e)
GLA_CHUNK = 64
GLA_GROUP = 4
GLA_STEP = 512
NEG_BIG = -1e30


def _nt_dot(a, b, precision=None):
    return lax.dot_general(a, b, (((1,), (1,)), ((), ())),
                           preferred_element_type=F32, precision=precision)


def _tn_dot(a, b, precision=None):
    return lax.dot_general(a, b, (((0,), (0,)), ((), ())),
                           preferred_element_type=F32, precision=precision)


def _split3(x):
    x1 = x.astype(BF16)
    r1 = x - x1.astype(F32)
    x2 = r1.astype(BF16)
    x3 = (r1 - x2.astype(F32)).astype(BF16)
    return x1, x2, x3


def _const_spec(shape):
    nd = len(shape)
    return pl.BlockSpec(shape, lambda *_: (0,) * nd, pipeline_mode=pl.Buffered(1))


def _params(*sem):
    return pltpu.CompilerParams(dimension_semantics=sem, vmem_limit_bytes=VMEM_LIMIT_BYTES)


def _layer_norm_rows(y, g, b):
    mu = jnp.mean(y, axis=-1, keepdims=True)
    d = y - mu
    var = jnp.mean(d * d, axis=-1, keepdims=True)
    return d * lax.rsqrt(var + LN_EPS) * g + b


def _rope_tables(seq, group):
    half = group // 2
    inv = 1.0 / (ROPE_THETA ** (jnp.arange(0, group, 2, dtype=F32) / group))
    ang = jnp.arange(seq, dtype=F32)[:, None] * inv[None, :]
    cos, sin = jnp.cos(ang), jnp.sin(ang)
    cos_g = jnp.concatenate([cos, cos], axis=-1)
    sin_g = jnp.concatenate([-sin, sin], axis=-1)
    reps = LANES // group
    return jnp.tile(cos_g, (1, reps)), jnp.tile(sin_g, (1, reps))


def _rope(acc, cos, sin, group):
    width = acc.shape[1]
    half = group // 2
    lane = lax.broadcasted_iota(jnp.int32, acc.shape, 1)
    upper = pltpu.roll(acc, width - half, axis=1)
    lower = pltpu.roll(acc, half, axis=1)
    partner = jnp.where((lane % group) < half, upper, lower)
    reps = width // LANES
    c = jnp.concatenate([cos] * reps, axis=1)
    s = jnp.concatenate([sin] * reps, axis=1)
    return acc * c + partner * s


def _hy_inproj_kernel(x_ref, w_ref, cd_ref, sd_ref, cm_ref, sm_ref, h_ref, km_ref):
    xb = x_ref[...].astype(BF16)
    rows = xb.shape[0]
    n_chunks = w_ref.shape[1] // COL_CHUNK
    for c in range(n_chunks):
        cols = slice(c * COL_CHUNK, (c + 1) * COL_CHUNK)
        acc = jnp.dot(xb, w_ref[:, cols], preferred_element_type=F32)
        if c == 0:
            acc = _rope(acc, cd_ref[...], sd_ref[...], DIFF_QK_DIM) * (DIFF_QK_DIM ** -0.5 * LOG2E)
        elif c == 1:
            acc = _rope(acc, cd_ref[...], sd_ref[...], DIFF_QK_DIM)
        elif c == 3:
            acc = _rope(acc, cm_ref[...], sm_ref[...], MOBA_HEAD_DIM) * (MOBA_HEAD_DIM ** -0.5 * LOG2E)
        elif c == 4:
            acc = _rope(acc, cm_ref[...], sm_ref[...], MOBA_HEAD_DIM)
        if c == 4:
            blocks = rows // MOBA_BLOCK
            km_ref[0] = jnp.mean(acc.reshape(blocks, MOBA_BLOCK, COL_CHUNK), axis=1)
        h_ref[:, cols] = acc.astype(BF16)


def _hy_inproj(x2, w, tabs, seq):
    tokens, d = x2.shape
    width = w.shape[1]
    assert width == 6 * COL_CHUNK and seq % ROW_TILE == 0 and ROW_TILE % MOBA_BLOCK == 0
    steps = tokens // ROW_TILE
    per_seq = seq // ROW_TILE
    blocks = ROW_TILE // MOBA_BLOCK
    tab_spec = pl.BlockSpec((ROW_TILE, LANES), lambda i: (i % per_seq, 0))
    return pl.pallas_call(
        _hy_inproj_kernel,
        grid=(steps,),
        in_specs=[pl.BlockSpec((ROW_TILE, d), lambda i: (i, 0)),
                  _const_spec((d, width)),
                  tab_spec, tab_spec, tab_spec, tab_spec],
        out_specs=[pl.BlockSpec((ROW_TILE, width), lambda i: (i, 0)),
                   pl.BlockSpec((1, blocks, COL_CHUNK), lambda i: (i, 0, 0))],
        out_shape=[jax.ShapeDtypeStruct((tokens, width), BF16),
                   jax.ShapeDtypeStruct((steps, blocks, COL_CHUNK), F32)],
        compiler_params=_params("arbitrary"),
        name="hy_inproj",
    )(x2, w, *tabs)


def _causal_pairs(n_q, ratio):
    qi = np.array([q for q in range(n_q) for _ in range(q // ratio + 1)], np.int32)
    ki = np.array([k for q in range(n_q) for k in range(q // ratio + 1)], np.int32)
    return jnp.asarray(qi), jnp.asarray(ki)


def _for_each_tile_kind(qi, ki, tq, tk, step, finish):
    ratio = tk // tq

    @pl.when(ki < qi // ratio)
    def _full():
        step(None)

    for part in range(ratio):
        @pl.when((ki == qi // ratio) & (qi % ratio == part))
        def _diag(part=part):
            step(part * tq)
            finish()


def _with_ones(v):
    return jnp.concatenate([v, jnp.ones(v.shape, v.dtype)], axis=1)


def _flash_update(s, v_ext, m_ref, acc_ref, rows):
    reps = s.shape[1] // LANES
    m_prev = m_ref[rows, :]
    m_new = jnp.maximum(m_prev, jnp.max(s, axis=-1, keepdims=True))
    alpha = jnp.exp2(m_prev - m_new)
    p = jnp.exp2(s - jnp.concatenate([m_new] * reps, axis=1))
    pv = jnp.dot(p.astype(BF16), v_ext, preferred_element_type=F32)
    acc_ref[rows, :] = jnp.concatenate([alpha, alpha], axis=1) * acc_ref[rows, :] + pv
    m_ref[rows, :] = m_new


def _attend_tile(s_all, v, m_ref, acc_ref, row0, tq, diag_offset):
    rb = ATTN_ROWS
    v_ext = _with_ones(v)
    for r in range(s_all.shape[0] // rb):
        rows = slice(row0 + r * rb, row0 + (r + 1) * rb)
        s = s_all[r * rb:(r + 1) * rb]
        if diag_offset is not None:
            first = diag_offset + (r * rb) % tq
            keys = first + rb
            row = lax.broadcasted_iota(jnp.int32, (rb, keys), 0) + first
            col = lax.broadcasted_iota(jnp.int32, (rb, keys), 1)
            s = jnp.where(col <= row, s[:, 0:keys], NEG_BIG)
            _flash_update(s, v_ext[0:keys], m_ref, acc_ref, rows)
        else:
            _flash_update(s, v_ext, m_ref, acc_ref, rows)


def _diff_attn_kernel(qi_tab, ki_tab, q_ref, k_ref, v_ref, lam_ref, sub_ref, o_ref,
                      qs_scr, m_scr, acc_scr, *, lambda_init):
    p_id = pl.program_id(2)
    qi = qi_tab[p_id]
    ki = ki_tab[p_id]
    tq = q_ref.shape[1]
    heads = q_ref.shape[2] // LANES
    dv = DIFF_V_DIM

    @pl.when(ki == 0)
    def _init():
        for hh in range(heads):
            q = q_ref[0, :, hh * LANES:(hh + 1) * LANES]
            lane = lax.broadcasted_iota(jnp.int32, q.shape, 1)
            zero = jnp.zeros_like(q)
            base = hh * 2 * tq
            qs_scr[base:base + tq, :] = jnp.where(lane < DIFF_QK_DIM, q, zero)
            qs_scr[base + tq:base + 2 * tq, :] = jnp.where(lane >= DIFF_QK_DIM, q, zero)
        m_scr[...] = jnp.full(m_scr.shape, NEG_BIG, F32)
        acc_scr[...] = jnp.zeros(acc_scr.shape, F32)

    def step(diag_offset):
        scores = [_nt_dot(qs_scr[hh * 2 * tq:(hh + 1) * 2 * tq, :], k_ref[0, :, hh * LANES:(hh + 1) * LANES])
                  for hh in range(heads)]
        for hh in range(heads):
            _attend_tile(scores[hh], v_ref[0, :, hh * dv:(hh + 1) * dv], m_scr, acc_scr, hh * 2 * tq, tq,
                         diag_offset)

    def finish():
        lam = lam_ref[...].astype(F32)
        lam_full = (jnp.exp(jnp.sum(lam[0:1] * lam[1:2], axis=-1, keepdims=True))
                    - jnp.exp(jnp.sum(lam[2:3] * lam[3:4], axis=-1, keepdims=True))
                    + lambda_init)
        for hh in range(heads):
            base = hh * 2 * tq
            o1 = acc_scr[base:base + tq, 0:dv] / acc_scr[base:base + tq, dv:]
            o2 = acc_scr[base + tq:base + 2 * tq, 0:dv] / acc_scr[base + tq:base + 2 * tq, dv:]
            o = o1 - lam_full * o2
            y = o * lax.rsqrt(jnp.mean(o * o, axis=-1, keepdims=True) + RMS_EPS)
            o_ref[0, :, hh * dv:(hh + 1) * dv] = (y * sub_ref[...] * (1.0 - lambda_init)).astype(o_ref.dtype)

    _for_each_tile_kind(qi, ki, tq, k_ref.shape[1], step, finish)


def _diff_attention(h3, lam, subln, lambda_init):
    bsz, seq, _ = h3.shape
    t = ATTN_TILE
    tk = DIFF_KEY_TILE
    assert seq % tk == 0 and tk % t == 0
    qi_tab, ki_tab = _causal_pairs(seq // t, tk // t)
    hp = ATTN_HEADS_PER_STEP
    assert DIFF_HEADS % hp == 0 and 2 * DIFF_QK_DIM == LANES and DIFF_V_DIM == LANES
    groups = DIFF_HEADS // hp
    w = hp * LANES
    grid_spec = pltpu.PrefetchScalarGridSpec(
        num_scalar_prefetch=2,
        grid=(bsz, groups, int(qi_tab.shape[0])),
        in_specs=[
            pl.BlockSpec((1, t, w), lambda b, h, p, qt, kt: (b, qt[p], h)),
            pl.BlockSpec((1, tk, w), lambda b, h, p, qt, kt: (b, kt[p], groups + h)),
            pl.BlockSpec((1, tk, w), lambda b, h, p, qt, kt: (b, kt[p], 2 * groups + h)),
            pl.BlockSpec((4, DIFF_QK_DIM), lambda b, h, p, qt, kt: (0, 0)),
            pl.BlockSpec((1, DIFF_V_DIM), lambda b, h, p, qt, kt: (0, 0)),
        ],
        out_specs=pl.BlockSpec((1, t, w), lambda b, h, p, qt, kt: (b, qt[p], h)),
        scratch_shapes=[pltpu.VMEM((hp * 2 * t, LANES), BF16),
                        pltpu.VMEM((hp * 2 * t, LANES), F32),
                        pltpu.VMEM((hp * 2 * t, 2 * DIFF_V_DIM), F32)],
    )
    return pl.pallas_call(
        functools.partial(_diff_attn_kernel, lambda_init=lambda_init),
        grid_spec=grid_spec,
        out_shape=jax.ShapeDtypeStruct((bsz, seq, DIFF_HEADS * DIFF_V_DIM), BF16),
        compiler_params=_params("arbitrary", "arbitrary", "arbitrary"),
        name="diff_attn",
    )(qi_tab, ki_tab, h3, h3, h3, lam, subln.reshape(1, DIFF_V_DIM))


def _moba_kernel(qi_tab, ki_tab, q_ref, k_ref, v_ref, km_ref, oh_ref, o_ref,
                 qa_scr, m_scr, acc_scr, *, topk):
    p_id = pl.program_id(2)
    qi = qi_tab[p_id]
    ki = ki_tab[p_id]
    tq = q_ref.shape[1]
    d = MOBA_HEAD_DIM
    heads = q_ref.shape[2] // d

    @pl.when(ki == 0)
    def _init():
        nb = km_ref.shape[1]
        blk = lax.broadcasted_iota(jnp.int32, (nb, tq), 0)
        pos = lax.broadcasted_iota(jnp.int32, (nb, tq), 1)
        own = (qi * tq + pos) // MOBA_BLOCK
        for hh in range(heads):
            q = q_ref[0, :, hh * d:(hh + 1) * d]
            km3 = jnp.concatenate(_split3(km_ref[0, :, hh * d:(hh + 1) * d]), axis=0)
            g3 = _nt_dot(km3, q)
            gate = g3[0:nb] + g3[nb:2 * nb] + g3[2 * nb:3 * nb]
            g = jnp.where(blk < own, gate, -jnp.inf)
            sel = blk == own
            for _ in range(topk):
                mx = jnp.max(g, axis=0, keepdims=True)
                idx = jnp.min(jnp.where(g == mx, blk, nb), axis=0, keepdims=True)
                sel = sel | ((blk == idx) & (mx > -jnp.inf))
                g = jnp.where(blk == idx, -jnp.inf, g)
            bias_t = jnp.concatenate([jnp.where(sel, 0.0, NEG_BIG),
                                      jnp.full((LANES - nb, tq), NEG_BIG, F32)], axis=0)
            qa_scr[hh * tq:(hh + 1) * tq, 0:d] = q
            qa_scr[hh * tq:(hh + 1) * tq, d:2 * d] = jnp.transpose(bias_t).astype(BF16)
        m_scr[...] = jnp.full(m_scr.shape, NEG_BIG, F32)
        acc_scr[...] = jnp.zeros(acc_scr.shape, F32)

    def step(diag_offset):
        onehot = oh_ref[...]
        scores = [_nt_dot(qa_scr[hh * tq:(hh + 1) * tq, :],
                          jnp.concatenate([k_ref[0, :, hh * d:(hh + 1) * d], onehot], axis=1))
                  for hh in range(heads)]
        for hh in range(heads):
            _attend_tile(scores[hh], v_ref[0, :, hh * d:(hh + 1) * d], m_scr, acc_scr, hh * tq, tq, diag_offset)

    def finish():
        for hh in range(heads):
            rows = slice(hh * tq, (hh + 1) * tq)
            o_ref[0, :, hh * d:(hh + 1) * d] = (acc_scr[rows, 0:d] / acc_scr[rows, d:2 * d]).astype(o_ref.dtype)

    _for_each_tile_kind(qi, ki, tq, k_ref.shape[1], step, finish)


def _moba_attention(h3, kmean):
    bsz, seq, _ = h3.shape
    t = ATTN_TILE
    tk = MOBA_KEY_TILE
    nb = seq // MOBA_BLOCK
    assert seq % tk == 0 and tk % t == 0 and t % MOBA_BLOCK == 0 and nb <= LANES and MOBA_HEAD_DIM == LANES
    qi_tab, ki_tab = _causal_pairs(seq // t, tk // t)
    hp = ATTN_HEADS_PER_STEP
    assert MOBA_HEADS % hp == 0
    groups = MOBA_HEADS // hp
    w = hp * MOBA_HEAD_DIM
    q0 = (3 * DIFF_HEADS * DIFF_V_DIM) // w
    k0 = q0 + groups
    v0 = k0 + groups
    block_of_key = jnp.arange(seq, dtype=jnp.int32)[:, None] // MOBA_BLOCK
    onehot = (block_of_key == jnp.arange(LANES, dtype=jnp.int32)[None, :]).astype(BF16)
    grid_spec = pltpu.PrefetchScalarGridSpec(
        num_scalar_prefetch=2,
        grid=(bsz, groups, int(qi_tab.shape[0])),
        in_specs=[
            pl.BlockSpec((1, t, w), lambda b, h, p, qt, kt: (b, qt[p], q0 + h)),
            pl.BlockSpec((1, tk, w), lambda b, h, p, qt, kt: (b, kt[p], k0 + h)),
            pl.BlockSpec((1, tk, w), lambda b, h, p, qt, kt: (b, kt[p], v0 + h)),
            pl.BlockSpec((1, nb, w), lambda b, h, p, qt, kt: (b, 0, h)),
            pl.BlockSpec((tk, LANES), lambda b, h, p, qt, kt: (kt[p], 0)),
        ],
        out_specs=pl.BlockSpec((1, t, w), lambda b, h, p, qt, kt: (b, qt[p], h)),
        scratch_shapes=[pltpu.VMEM((hp * t, 2 * MOBA_HEAD_DIM), BF16),
                        pltpu.VMEM((hp * t, LANES), F32),
                        pltpu.VMEM((hp * t, 2 * MOBA_HEAD_DIM), F32)],
    )
    return pl.pallas_call(
        functools.partial(_moba_kernel, topk=min(MOBA_TOPK, nb)),
        grid_spec=grid_spec,
        out_shape=jax.ShapeDtypeStruct((bsz, seq, MOBA_HEADS * MOBA_HEAD_DIM), BF16),
        compiler_params=_params("arbitrary", "arbitrary", "arbitrary"),
        name="moba_attn",
    )(qi_tab, ki_tab, h3, h3, h3, kmean, onehot)


def _mlp_rows(x, w1_ref, w2_ref, g, b, alpha):
    xb = x.astype(BF16)
    y = alpha * x
    for c in range(w1_ref.shape[1] // COL_CHUNK):
        cols = slice(c * COL_CHUNK, (c + 1) * COL_CHUNK)
        hid = jnp.maximum(jnp.dot(xb, w1_ref[:, cols], preferred_element_type=F32), 0.0)
        y = y + jnp.dot((hid * hid).astype(BF16), w2_ref[cols, :], preferred_element_type=F32)
    return _layer_norm_rows(y, g, b)


def _norm_then_mlp(y_halves, g1, b1, w1_ref, w2_ref, g2, b2, o_ref, alpha):
    mids = [_layer_norm_rows(y, g1, b1) for y in y_halves]
    row0 = 0
    for mid in mids:
        o_ref[row0:row0 + mid.shape[0], :] = _mlp_rows(mid, w1_ref, w2_ref, g2, b2, alpha)
        row0 += mid.shape[0]


def _row_halves(rows):
    half = rows // OUT_MLP_SPLIT
    return [slice(i * half, (i + 1) * half) for i in range(OUT_MLP_SPLIT)]


def _hy_out_mlp_kernel(a_ref, m_ref, wa_ref, wm_ref, x_ref, g1_ref, b1_ref, w1_ref, w2_ref, g2_ref, b2_ref,
                       o_ref, *, alpha):
    ys = []
    for rows in _row_halves(x_ref.shape[0]):
        y = alpha * x_ref[rows, :]
        y = y + jnp.dot(a_ref[rows, :], wa_ref[...], preferred_element_type=F32)
        ys.append(y + jnp.dot(m_ref[rows, :], wm_ref[...], preferred_element_type=F32))
    _norm_then_mlp(ys, g1_ref[...], b1_ref[...], w1_ref, w2_ref, g2_ref[...], b2_ref[...], o_ref, alpha)


def _hy_out_mlp(a2, m2, w_out, x2, g1, b1, w1, w2, g2, b2, alpha):
    tokens, d = x2.shape
    wa = a2.shape[1]
    wm = m2.shape[1]
    dff = w1.shape[1]
    t = OUT_MLP_TILE
    assert tokens % t == 0
    row = lambda width: pl.BlockSpec((t, width), lambda i: (i, 0))
    vec = _const_spec((1, d))
    return pl.pallas_call(
        functools.partial(_hy_out_mlp_kernel, alpha=alpha),
        grid=(tokens // t,),
        in_specs=[row(wa), row(wm), _const_spec((wa, d)), _const_spec((wm, d)), row(d), vec, vec,
                  _const_spec((d, dff)), _const_spec((dff, d)), vec, vec],
        out_specs=row(d),
        out_shape=jax.ShapeDtypeStruct((tokens, d), F32),
        compiler_params=_params("arbitrary"),
        name="hy_out_mlp",
    )(a2, m2, w_out[:wa], w_out[wa:], x2, g1.reshape(1, d), b1.reshape(1, d), w1, w2,
      g2.reshape(1, d), b2.reshape(1, d))


def _gla_out_mlp_kernel(o_in_ref, r_ref, w_ref, x_ref, g1_ref, b1_ref, w1_ref, w2_ref, g2_ref, b2_ref,
                        o_ref, *, alpha):
    ys = []
    for rows in _row_halves(x_ref.shape[0]):
        r = r_ref[rows, :].astype(F32)
        gated = o_in_ref[rows, :].astype(F32) * (r * jax.nn.sigmoid(r))
        ys.append(alpha * x_ref[rows, :] + jnp.dot(gated.astype(BF16), w_ref[...], preferred_element_type=F32))
    _norm_then_mlp(ys, g1_ref[...], b1_ref[...], w1_ref, w2_ref, g2_ref[...], b2_ref[...], o_ref, alpha)


def _gla_out_mlp(o2, h2, w_out, x2, g1, b1, w1, w2, g2, b2, alpha):
    tokens, d = x2.shape
    vw = o2.shape[1]
    dff = w1.shape[1]
    t = OUT_MLP_TILE
    r_block = (h2.shape[1] - vw) // vw
    assert r_block * vw + vw == h2.shape[1] and tokens % t == 0
    row = lambda width: pl.BlockSpec((t, width), lambda i: (i, 0))
    vec = _const_spec((1, d))
    return pl.pallas_call(
        functools.partial(_gla_out_mlp_kernel, alpha=alpha),
        grid=(tokens // t,),
        in_specs=[row(vw), pl.BlockSpec((t, vw), lambda i: (i, r_block)),
                  _const_spec((vw, d)), row(d), vec, vec,
                  _const_spec((d, dff)), _const_spec((dff, d)), vec, vec],
        out_specs=row(d),
        out_shape=jax.ShapeDtypeStruct((tokens, d), F32),
        compiler_params=_params("arbitrary"),
        name="gla_out_mlp",
    )(o2, h2, w_out, x2, g1.reshape(1, d), b1.reshape(1, d), w1, w2, g2.reshape(1, d), b2.reshape(1, d))


def _gla_inproj_kernel(x_ref, w_ref, wd_ref, wu_ref, bg_ref, h_ref, g_ref):
    xb = x_ref[...].astype(BF16)
    low = jnp.dot(xb, wd_ref[...], preferred_element_type=F32)
    z = jnp.dot(low.astype(BF16), wu_ref[...], preferred_element_type=F32) + bg_ref[...]
    log_sig = jnp.minimum(z, 0.0) - jnp.log1p(jnp.exp(-jnp.abs(z)))
    g_ref[...] = log_sig / GLA_GATE_NORM
    for c in range(w_ref.shape[1] // COL_CHUNK):
        cols = slice(c * COL_CHUNK, (c + 1) * COL_CHUNK)
        h_ref[:, cols] = jnp.dot(xb, w_ref[:, cols], preferred_element_type=F32).astype(BF16)


def _gla_inproj(x2, w_main, w_down, w_up, b_gate):
    tokens, d = x2.shape
    width = w_main.shape[1]
    kw = w_up.shape[1]
    row = lambda w_: pl.BlockSpec((ROW_TILE, w_), lambda i: (i, 0))
    return pl.pallas_call(
        _gla_inproj_kernel,
        grid=(tokens // ROW_TILE,),
        in_specs=[row(d), _const_spec((d, width)), _const_spec((d, LANES)),
                  _const_spec((LANES, kw)), _const_spec((1, kw))],
        out_specs=[row(width), row(kw)],
        out_shape=[jax.ShapeDtypeStruct((tokens, width), BF16),
                   jax.ShapeDtypeStruct((tokens, kw), F32)],
        compiler_params=_params("arbitrary"),
        name="gla_inproj",
    )(x2, w_main, w_down, w_up, b_gate.reshape(1, kw))


def _gla_level_ref(row_of, dk, half, rows):
    span = 2 * half
    pieces = []
    if span >= SUBLANES:
        for p in range(rows // span):
            r = p * span + half - 1
            pieces.append(jnp.broadcast_to(row_of(r), (span, dk)))
    else:
        sub = lax.broadcasted_iota(jnp.int32, (SUBLANES, dk), 0)
        for base in range(0, rows, SUBLANES):
            piece = None
            for p in range(SUBLANES // span):
                r = base + p * span + half - 1
                bc = jnp.broadcast_to(row_of(r), (SUBLANES, dk))
                piece = bc if piece is None else jnp.where(sub >= p * span, bc, piece)
            pieces.append(piece)
    return jnp.concatenate(pieces, axis=0)


def _gla_kernel(q_ref, k_ref, v_ref, g_ref, ng_ref, o_ref, st_ref, b_scr, *, scale):
    L = GLA_CHUNK
    dk, dv, heads = GLA_DK, GLA_DV, GLA_HEADS
    n_chunks = q_ref.shape[1] // L
    levels = [L >> (i + 1) for i in range(L.bit_length() - 1)]

    @pl.when(pl.program_id(1) == 0)
    def _reset():
        st_ref[...] = jnp.zeros(st_ref.shape, F32)

    r_i = lax.broadcasted_iota(jnp.int32, (L, L), 0)
    c_i = lax.broadcasted_iota(jnp.int32, (L, L), 1)
    tri = (c_i <= r_i).astype(BF16)
    level_of = jnp.where(c_i == r_i, -1, -2)
    for idx, half in enumerate(levels):
        span = 2 * half
        member = ((r_i // span) == (c_i // span)) & ((r_i % span) >= half) & ((c_i % span) < half)
        level_of = jnp.where(member, idx, level_of)

    def chunk_group(gi, carry):
        hs = range(heads)
        cs = range(GLA_GROUP)
        kcols = [slice(h * dk, (h + 1) * dk) for h in hs]
        vcols = [slice(h * dv, (h + 1) * dv) for h in hs]
        rows = [pl.ds(pl.multiple_of((gi * GLA_GROUP + c) * L, L), L) for c in cs]
        b = {}
        for c in cs:
            for h in hs:
                cs3 = jnp.dot(tri, jnp.concatenate(_split3(g_ref[0, rows[c], kcols[h]]), axis=1),
                              preferred_element_type=F32)
                b[c, h] = cs3[:, 0:dk] + cs3[:, dk:2 * dk] + cs3[:, 2 * dk:3 * dk]
                b_scr[c, h] = b[c, h]
        qb = {(c, h): q_ref[0, rows[c], kcols[h]] for c in cs for h in hs}
        kb = {(c, h): k_ref[0, rows[c], kcols[h]] for c in cs for h in hs}
        v = {(c, h): v_ref[0, rows[c], vcols[h]] for c in cs for h in hs}
        qf = {key: qb[key].astype(F32) * scale for key in qb}
        kf = {key: kb[key].astype(F32) for key in kb}

        att = {key: jnp.where(level_of == -1, _nt_dot(qb[key], kb[key]) * scale, 0.0) for key in qb}
        for idx, half in enumerate(levels):
            for c in cs:
                for h in hs:
                    ref = _gla_level_ref(lambda r, c=c, h=h: b_scr[c, h, r:r + 1, :], dk, half, L)
                    e = jnp.exp(-jnp.abs(b[c, h] - ref))
                    a = _nt_dot((qf[c, h] * e).astype(BF16), (kf[c, h] * e).astype(BF16))
                    att[c, h] = jnp.where(level_of == idx, a, att[c, h])

        state = [st_ref[h] for h in hs]
        o = {}
        for c in cs:
            for h in hs:
                o[c, h] = jnp.dot((qf[c, h] * jnp.exp(b[c, h])).astype(BF16), state[h].astype(BF16),
                                  preferred_element_type=F32)
            for h in hs:
                b_last = b_scr[c, h, L - 1:L, :]
                kd = (kf[c, h] * jnp.exp(b_last - b[c, h])).astype(BF16)
                decay = jnp.transpose(jnp.broadcast_to(jnp.exp(b_last), (dk, dk)))
                decay = jnp.concatenate([decay] * (dv // dk), axis=1)
                state[h] = decay * state[h] + _tn_dot(kd, v[c, h])
        for h in hs:
            st_ref[h] = state[h]

        for c in cs:
            for h in hs:
                oh = o[c, h] + jnp.dot(att[c, h].astype(BF16), v[c, h], preferred_element_type=F32)
                y = oh * lax.rsqrt(jnp.mean(oh * oh, axis=-1, keepdims=True) + RMS_EPS)
                o_ref[0, rows[c], vcols[h]] = (y * ng_ref[...]).astype(o_ref.dtype)
        return carry

    lax.fori_loop(0, n_chunks // GLA_GROUP, chunk_group, 0)


def _gla_recurrence(h3, g3, norm_g):
    bsz, seq, _ = h3.shape
    t = GLA_STEP
    assert seq % t == 0 and t % (GLA_CHUNK * GLA_GROUP) == 0 and GLA_DV % GLA_DK == 0
    kw = GLA_HEADS * GLA_DK
    vw = GLA_HEADS * GLA_DV
    assert (2 * kw) % vw == 0
    return pl.pallas_call(
        functools.partial(_gla_kernel, scale=GLA_DK ** -0.5),
        grid=(bsz, seq // t),
        in_specs=[
            pl.BlockSpec((1, t, kw), lambda b, s: (b, s, 0)),
            pl.BlockSpec((1, t, kw), lambda b, s: (b, s, 1)),
            pl.BlockSpec((1, t, vw), lambda b, s: (b, s, (2 * kw) // vw)),
            pl.BlockSpec((1, t, kw), lambda b, s: (b, s, 0)),
            pl.BlockSpec((1, GLA_DV), lambda b, s: (0, 0)),
        ],
        out_specs=pl.BlockSpec((1, t, vw), lambda b, s: (b, s, 0)),
        out_shape=jax.ShapeDtypeStruct((bsz, seq, vw), BF16),
        scratch_shapes=[pltpu.VMEM((GLA_HEADS, GLA_DK, GLA_DV), F32),
                        pltpu.VMEM((GLA_GROUP, GLA_HEADS, GLA_CHUNK, GLA_DK), F32)],
        compiler_params=_params("arbitrary", "arbitrary"),
        name="gla_recurrence",
    )(h3, h3, h3, g3, norm_g.reshape(1, GLA_DV))


def kernel(x, hy_w_in, diff_lambda, diff_subln, hy_w_out, gla_w_in, gla_w_gate_up, gla_b_gate,
           gla_norm, gla_w_out, ln_mix_g, ln_mix_b, ffn_w1, ffn_w2, ln_ffn_g, ln_ffn_b):
    bsz, seq, d = x.shape
    depth = ln_mix_g.shape[0]
    alpha = (2 * depth) ** 0.25
    tokens = bsz * seq
    assert tokens % ROW_TILE == 0

    tabs = _rope_tables(seq, DIFF_QK_DIM) + _rope_tables(seq, MOBA_HEAD_DIM)
    gla_main = 2 * GLA_HEADS * GLA_DK + 2 * GLA_HEADS * GLA_DV

    x2 = x.reshape(tokens, d)
    for layer in range(depth):
        w1 = ffn_w1[layer].astype(BF16)
        w2 = ffn_w2[layer].astype(BF16)
        if layer % 2 == 0:
            e = layer // 2
            lambda_init = 0.8 - 0.6 * math.exp(-0.3 * layer)
            h2, kmean = _hy_inproj(x2, hy_w_in[e].astype(BF16), tabs, seq)
            h3 = h2.reshape(bsz, seq, h2.shape[1])
            kmean = kmean.reshape(bsz, seq // MOBA_BLOCK, kmean.shape[-1])
            a = _diff_attention(h3, diff_lambda[e], diff_subln[e], lambda_init)
            m = _moba_attention(h3, kmean)
            x2 = _hy_out_mlp(a.reshape(tokens, -1), m.reshape(tokens, -1), hy_w_out[e].astype(BF16), x2,
                             ln_mix_g[layer], ln_mix_b[layer], w1, w2, ln_ffn_g[layer], ln_ffn_b[layer], alpha)
        else:
            o = layer // 2
            w_in = gla_w_in[o]
            w_down = jnp.pad(w_in[:, gla_main:], ((0, 0), (0, LANES - GLA_GATE_RANK))).astype(BF16)
            w_up = jnp.pad(gla_w_gate_up[o], ((0, LANES - GLA_GATE_RANK), (0, 0))).astype(BF16)
            h2, g2 = _gla_inproj(x2, w_in[:, :gla_main].astype(BF16), w_down, w_up, gla_b_gate[o])
            h3 = h2.reshape(bsz, seq, h2.shape[1])
            g3 = g2.reshape(bsz, seq, g2.shape[1])
            og = _gla_recurrence(h3, g3, gla_norm[o])
            x2 = _gla_out_mlp(og.reshape(tokens, -1), h2, gla_w_out[o].astype(BF16), x2,
                              ln_mix_g[layer], ln_mix_b[layer], w1, w2, ln_ffn_g[layer], ln_ffn_b[layer], alpha)
    return x2.reshape(bsz, seq, d)
```

```python
import functools
import math

import jax
import jax.numpy as jnp
import numpy as np
from jax import lax
from jax.experimental import pallas as pl
from jax.experimental.pallas import tpu as pltpu

F32 = jnp.float32
BF16 = jnp.bfloat16

DIFF_HEADS = 4
DIFF_QK_DIM = 64
DIFF_V_DIM = 128
MOBA_HEADS = 4
MOBA_HEAD_DIM = 128
MOBA_BLOCK = 256
MOBA_TOPK = 3
GLA_HEADS = 4
GLA_DK = 128
GLA_DV = 256
GLA_GATE_RANK = 16
GLA_GATE_NORM = 16.0
ROPE_THETA = 10000.0
LN_EPS = 1e-5
RMS_EPS = 1e-5

LANES = 128
SUBLANES = 8
VMEM_LIMIT_BYTES = 56 * 1024 * 1024

ROW_TILE = 1024
COL_CHUNK = 512
OUT_MLP_TILE = 1024
OUT_MLP_SPLIT = 2
ATTN_TILE = 512
DIFF_KEY_TILE = 1024
MOBA_KEY_TILE = 2048
ATTN_ROWS = 128
ATTN_HEADS_PER_STEP = 4
LOG2E = math.log2(math.e)
GLA_CHUNK = 64
GLA_GROUP = 4
GLA_STEP = 512
NEG_BIG = -1e30


def _nt_dot(a, b, precision=None):
    return lax.dot_general(a, b, (((1,), (1,)), ((), ())),
                           preferred_element_type=F32, precision=precision)


def _tn_dot(a, b, precision=None):
    return lax.dot_general(a, b, (((0,), (0,)), ((), ())),
                           preferred_element_type=F32, precision=precision)


def _split3(x):
    x1 = x.astype(BF16)
    r1 = x - x1.astype(F32)
    x2 = r1.astype(BF16)
    x3 = (r1 - x2.astype(F32)).astype(BF16)
    return x1, x2, x3


def _const_spec(shape):
    nd = len(shape)
    return pl.BlockSpec(shape, lambda *_: (0,) * nd, pipeline_mode=pl.Buffered(1))


def _params(*sem):
    return pltpu.CompilerParams(dimension_semantics=sem, vmem_limit_bytes=VMEM_LIMIT_BYTES)


def _layer_norm_rows(y, g, b):
    mu = jnp.mean(y, axis=-1, keepdims=True)
    d = y - mu
    var = jnp.mean(d * d, axis=-1, keepdims=True)
    return d * lax.rsqrt(var + LN_EPS) * g + b


def _rope_tables(seq, group):
    half = group // 2
    inv = 1.0 / (ROPE_THETA ** (jnp.arange(0, group, 2, dtype=F32) / group))
    ang = jnp.arange(seq, dtype=F32)[:, None] * inv[None, :]
    cos, sin = jnp.cos(ang), jnp.sin(ang)
    cos_g = jnp.concatenate([cos, cos], axis=-1)
    sin_g = jnp.concatenate([-sin, sin], axis=-1)
    reps = LANES // group
    return jnp.tile(cos_g, (1, reps)), jnp.tile(sin_g, (1, reps))


def _rope(acc, cos, sin, group):
    width = acc.shape[1]
    half = group // 2
    lane = lax.broadcasted_iota(jnp.int32, acc.shape, 1)
    upper = pltpu.roll(acc, width - half, axis=1)
    lower = pltpu.roll(acc, half, axis=1)
    partner = jnp.where((lane % group) < half, upper, lower)
    reps = width // LANES
    c = jnp.concatenate([cos] * reps, axis=1)
    s = jnp.concatenate([sin] * reps, axis=1)
    return acc * c + partner * s


def _hy_inproj_kernel(x_ref, w_ref, cd_ref, sd_ref, cm_ref, sm_ref, h_ref, km_ref):
    xb = x_ref[...].astype(BF16)
    rows = xb.shape[0]
    n_chunks = w_ref.shape[1] // COL_CHUNK
    for c in range(n_chunks):
        cols = slice(c * COL_CHUNK, (c + 1) * COL_CHUNK)
        acc = jnp.dot(xb, w_ref[:, cols], preferred_element_type=F32)
        if c == 0:
            acc = _rope(acc, cd_ref[...], sd_ref[...], DIFF_QK_DIM) * (DIFF_QK_DIM ** -0.5 * LOG2E)
        elif c == 1:
            acc = _rope(acc, cd_ref[...], sd_ref[...], DIFF_QK_DIM)
        elif c == 3:
            acc = _rope(acc, cm_ref[...], sm_ref[...], MOBA_HEAD_DIM) * (MOBA_HEAD_DIM ** -0.5 * LOG2E)
        elif c == 4:
            acc = _rope(acc, cm_ref[...], sm_ref[...], MOBA_HEAD_DIM)
        if c == 4:
            blocks = rows // MOBA_BLOCK
            km_ref[0] = jnp.mean(acc.reshape(blocks, MOBA_BLOCK, COL_CHUNK), axis=1)
        h_ref[:, cols] = acc.astype(BF16)


def _hy_inproj(x2, w, tabs, seq):
    tokens, d = x2.shape
    width = w.shape[1]
    assert width == 6 * COL_CHUNK and seq % ROW_TILE == 0 and ROW_TILE % MOBA_BLOCK == 0
    steps = tokens // ROW_TILE
    per_seq = seq // ROW_TILE
    blocks = ROW_TILE // MOBA_BLOCK
    tab_spec = pl.BlockSpec((ROW_TILE, LANES), lambda i: (i % per_seq, 0))
    return pl.pallas_call(
        _hy_inproj_kernel,
        grid=(steps,),
        in_specs=[pl.BlockSpec((ROW_TILE, d), lambda i: (i, 0)),
                  _const_spec((d, width)),
                  tab_spec, tab_spec, tab_spec, tab_spec],
        out_specs=[pl.BlockSpec((ROW_TILE, width), lambda i: (i, 0)),
                   pl.BlockSpec((1, blocks, COL_CHUNK), lambda i: (i, 0, 0))],
        out_shape=[jax.ShapeDtypeStruct((tokens, width), BF16),
                   jax.ShapeDtypeStruct((steps, blocks, COL_CHUNK), F32)],
        compiler_params=_params("arbitrary"),
        name="hy_inproj",
    )(x2, w, *tabs)


def _causal_pairs(n_q, ratio):
    qi = np.array([q for q in range(n_q) for _ in range(q // ratio + 1)], np.int32)
    ki = np.array([k for q in range(n_q) for k in range(q // ratio + 1)], np.int32)
    return jnp.asarray(qi), jnp.asarray(ki)


def _for_each_tile_kind(qi, ki, tq, tk, step, finish):
    ratio = tk // tq

    @pl.when(ki < qi // ratio)
    def _full():
        step(None)

    for part in range(ratio):
        @pl.when((ki == qi // ratio) & (qi % ratio == part))
        def _diag(part=part):
            step(part * tq)
            finish()


def _with_ones(v):
    return jnp.concatenate([v, jnp.ones(v.shape, v.dtype)], axis=1)


def _flash_update(s, v_ext, m_ref, acc_ref, rows):
    reps = s.shape[1] // LANES
    m_prev = m_ref[rows, :]
    m_new = jnp.maximum(m_prev, jnp.max(s, axis=-1, keepdims=True))
    alpha = jnp.exp2(m_prev - m_new)
    p = jnp.exp2(s - jnp.concatenate([m_new] * reps, axis=1))
    pv = jnp.dot(p.astype(BF16), v_ext, preferred_element_type=F32)
    acc_ref[rows, :] = jnp.concatenate([alpha, alpha], axis=1) * acc_ref[rows, :] + pv
    m_ref[rows, :] = m_new


def _attend_tile(s_all, v, m_ref, acc_ref, row0, tq, diag_offset):
    rb = ATTN_ROWS
    v_ext = _with_ones(v)
    for r in range(s_all.shape[0] // rb):
        rows = slice(row0 + r * rb, row0 + (r + 1) * rb)
        s = s_all[r * rb:(r + 1) * rb]
        if diag_offset is not None:
            first = diag_offset + (r * rb) % tq
            keys = first + rb
            row = lax.broadcasted_iota(jnp.int32, (rb, keys), 0) + first
            col = lax.broadcasted_iota(jnp.int32, (rb, keys), 1)
            s = jnp.where(col <= row, s[:, 0:keys], NEG_BIG)
            _flash_update(s, v_ext[0:keys], m_ref, acc_ref, rows)
        else:
            _flash_update(s, v_ext, m_ref, acc_ref, rows)


def _diff_attn_kernel(qi_tab, ki_tab, q_ref, k_ref, v_ref, lam_ref, sub_ref, o_ref,
                      qs_scr, m_scr, acc_scr, *, lambda_init):
    p_id = pl.program_id(2)
    qi = qi_tab[p_id]
    ki = ki_tab[p_id]
    tq = q_ref.shape[1]
    heads = q_ref.shape[2] // LANES
    dv = DIFF_V_DIM

    @pl.when(ki == 0)
    def _init():
        for hh in range(heads):
            q = q_ref[0, :, hh * LANES:(hh + 1) * LANES]
            lane = lax.broadcasted_iota(jnp.int32, q.shape, 1)
            zero = jnp.zeros_like(q)
            base = hh * 2 * tq
            qs_scr[base:base + tq, :] = jnp.where(lane < DIFF_QK_DIM, q, zero)
            qs_scr[base + tq:base + 2 * tq, :] = jnp.where(lane >= DIFF_QK_DIM, q, zero)
        m_scr[...] = jnp.full(m_scr.shape, NEG_BIG, F32)
        acc_scr[...] = jnp.zeros(acc_scr.shape, F32)

    def step(diag_offset):
        scores = [_nt_dot(qs_scr[hh * 2 * tq:(hh + 1) * 2 * tq, :], k_ref[0, :, hh * LANES:(hh + 1) * LANES])
                  for hh in range(heads)]
        for hh in range(heads):
            _attend_tile(scores[hh], v_ref[0, :, hh * dv:(hh + 1) * dv], m_scr, acc_scr, hh * 2 * tq, tq,
                         diag_offset)

    def finish():
        lam = lam_ref[...].astype(F32)
        lam_full = (jnp.exp(jnp.sum(lam[0:1] * lam[1:2], axis=-1, keepdims=True))
                    - jnp.exp(jnp.sum(lam[2:3] * lam[3:4], axis=-1, keepdims=True))
                    + lambda_init)
        for hh in range(heads):
            base = hh * 2 * tq
            o1 = acc_scr[base:base + tq, 0:dv] / acc_scr[base:base + tq, dv:]
            o2 = acc_scr[base + tq:base + 2 * tq, 0:dv] / acc_scr[base + tq:base + 2 * tq, dv:]
            o = o1 - lam_full * o2
            y = o * lax.rsqrt(jnp.mean(o * o, axis=-1, keepdims=True) + RMS_EPS)
            o_ref[0, :, hh * dv:(hh + 1) * dv] = (y * sub_ref[...] * (1.0 - lambda_init)).astype(o_ref.dtype)

    _for_each_tile_kind(qi, ki, tq, k_ref.shape[1], step, finish)


def _diff_attention(h3, lam, subln, lambda_init):
    bsz, seq, _ = h3.shape
    t = ATTN_TILE
    tk = DIFF_KEY_TILE
    assert seq % tk == 0 and tk % t == 0
    qi_tab, ki_tab = _causal_pairs(seq // t, tk // t)
    hp = ATTN_HEADS_PER_STEP
    assert DIFF_HEADS % hp == 0 and 2 * DIFF_QK_DIM == LANES and DIFF_V_DIM == LANES
    groups = DIFF_HEADS // hp
    w = hp * LANES
    grid_spec = pltpu.PrefetchScalarGridSpec(
        num_scalar_prefetch=2,
        grid=(bsz, groups, int(qi_tab.shape[0])),
        in_specs=[
            pl.BlockSpec((1, t, w), lambda b, h, p, qt, kt: (b, qt[p], h)),
            pl.BlockSpec((1, tk, w), lambda b, h, p, qt, kt: (b, kt[p], groups + h)),
            pl.BlockSpec((1, tk, w), lambda b, h, p, qt, kt: (b, kt[p], 2 * groups + h)),
            pl.BlockSpec((4, DIFF_QK_DIM), lambda b, h, p, qt, kt: (0, 0)),
            pl.BlockSpec((1, DIFF_V_DIM), lambda b, h, p, qt, kt: (0, 0)),
        ],
        out_specs=pl.BlockSpec((1, t, w), lambda b, h, p, qt, kt: (b, qt[p], h)),
        scratch_shapes=[pltpu.VMEM((hp * 2 * t, LANES), BF16),
                        pltpu.VMEM((hp * 2 * t, LANES), F32),
                        pltpu.VMEM((hp * 2 * t, 2 * DIFF_V_DIM), F32)],
    )
    return pl.pallas_call(
        functools.partial(_diff_attn_kernel, lambda_init=lambda_init),
        grid_spec=grid_spec,
        out_shape=jax.ShapeDtypeStruct((bsz, seq, DIFF_HEADS * DIFF_V_DIM), BF16),
        compiler_params=_params("arbitrary", "arbitrary", "arbitrary"),
        name="diff_attn",
    )(qi_tab, ki_tab, h3, h3, h3, lam, subln.reshape(1, DIFF_V_DIM))


def _moba_kernel(qi_tab, ki_tab, q_ref, k_ref, v_ref, km_ref, oh_ref, o_ref,
                 qa_scr, m_scr, acc_scr, *, topk):
    p_id = pl.program_id(2)
    qi = qi_tab[p_id]
    ki = ki_tab[p_id]
    tq = q_ref.shape[1]
    d = MOBA_HEAD_DIM
    heads = q_ref.shape[2] // d

    @pl.when(ki == 0)
    def _init():
        nb = km_ref.shape[1]
        blk = lax.broadcasted_iota(jnp.int32, (nb, tq), 0)
        pos = lax.broadcasted_iota(jnp.int32, (nb, tq), 1)
        own = (qi * tq + pos) // MOBA_BLOCK
        for hh in range(heads):
            q = q_ref[0, :, hh * d:(hh + 1) * d]
            km3 = jnp.concatenate(_split3(km_ref[0, :, hh * d:(hh + 1) * d]), axis=0)
            g3 = _nt_dot(km3, q)
            gate = g3[0:nb] + g3[nb:2 * nb] + g3[2 * nb:3 * nb]
            g = jnp.where(blk < own, gate, -jnp.inf)
            sel = blk == own
            for _ in range(topk):
                mx = jnp.max(g, axis=0, keepdims=True)
                idx = jnp.min(jnp.where(g == mx, blk, nb), axis=0, keepdims=True)
                sel = sel | ((blk == idx) & (mx > -jnp.inf))
                g = jnp.where(blk == idx, -jnp.inf, g)
            bias_t = jnp.concatenate([jnp.where(sel, 0.0, NEG_BIG),
                                      jnp.full((LANES - nb, tq), NEG_BIG, F32)], axis=0)
            qa_scr[hh * tq:(hh + 1) * tq, 0:d] = q
            qa_scr[hh * tq:(hh + 1) * tq, d:2 * d] = jnp.transpose(bias_t).astype(BF16)
        m_scr[...] = jnp.full(m_scr.shape, NEG_BIG, F32)
        acc_scr[...] = jnp.zeros(acc_scr.shape, F32)

    def step(diag_offset):
        onehot = oh_ref[...]
        scores = [_nt_dot(qa_scr[hh * tq:(hh + 1) * tq, :],
                          jnp.concatenate([k_ref[0, :, hh * d:(hh + 1) * d], onehot], axis=1))
                  for hh in range(heads)]
        for hh in range(heads):
            _attend_tile(scores[hh], v_ref[0, :, hh * d:(hh + 1) * d], m_scr, acc_scr, hh * tq, tq, diag_offset)

    def finish():
        for hh in range(heads):
            rows = slice(hh * tq, (hh + 1) * tq)
            o_ref[0, :, hh * d:(hh + 1) * d] = (acc_scr[rows, 0:d] / acc_scr[rows, d:2 * d]).astype(o_ref.dtype)

    _for_each_tile_kind(qi, ki, tq, k_ref.shape[1], step, finish)


def _moba_attention(h3, kmean):
    bsz, seq, _ = h3.shape
    t = ATTN_TILE
    tk = MOBA_KEY_TILE
    nb = seq // MOBA_BLOCK
    assert seq % tk == 0 and tk % t == 0 and t % MOBA_BLOCK == 0 and nb <= LANES and MOBA_HEAD_DIM == LANES
    qi_tab, ki_tab = _causal_pairs(seq // t, tk // t)
    hp = ATTN_HEADS_PER_STEP
    assert MOBA_HEADS % hp == 0
    groups = MOBA_HEADS // hp
    w = hp * MOBA_HEAD_DIM
    q0 = (3 * DIFF_HEADS * DIFF_V_DIM) // w
    k0 = q0 + groups
    v0 = k0 + groups
    block_of_key = jnp.arange(seq, dtype=jnp.int32)[:, None] // MOBA_BLOCK
    onehot = (block_of_key == jnp.arange(LANES, dtype=jnp.int32)[None, :]).astype(BF16)
    grid_spec = pltpu.PrefetchScalarGridSpec(
        num_scalar_prefetch=2,
        grid=(bsz, groups, int(qi_tab.shape[0])),
        in_specs=[
            pl.BlockSpec((1, t, w), lambda b, h, p, qt, kt: (b, qt[p], q0 + h)),
            pl.BlockSpec((1, tk, w), lambda b, h, p, qt, kt: (b, kt[p], k0 + h)),
            pl.BlockSpec((1, tk, w), lambda b, h, p, qt, kt: (b, kt[p], v0 + h)),
            pl.BlockSpec((1, nb, w), lambda b, h, p, qt, kt: (b, 0, h)),
            pl.BlockSpec((tk, LANES), lambda b, h, p, qt, kt: (kt[p], 0)),
        ],
        out_specs=pl.BlockSpec((1, t, w), lambda b, h, p, qt, kt: (b, qt[p], h)),
        scratch_shapes=[pltpu.VMEM((hp * t, 2 * MOBA_HEAD_DIM), BF16),
                        pltpu.VMEM((hp * t, LANES), F32),
                        pltpu.VMEM((hp * t, 2 * MOBA_HEAD_DIM), F32)],
    )
    return pl.pallas_call(
        functools.partial(_moba_kernel, topk=min(MOBA_TOPK, nb)),
        grid_spec=grid_spec,
        out_shape=jax.ShapeDtypeStruct((bsz, seq, MOBA_HEADS * MOBA_HEAD_DIM), BF16),
        compiler_params=_params("arbitrary", "arbitrary", "arbitrary"),
        name="moba_attn",
    )(qi_tab, ki_tab, h3, h3, h3, kmean, onehot)


def _mlp_rows(x, w1_ref, w2_ref, g, b, alpha):
    xb = x.astype(BF16)
    y = alpha * x
    for c in range(w1_ref.shape[1] // COL_CHUNK):
        cols = slice(c * COL_CHUNK, (c + 1) * COL_CHUNK)
        hid = jnp.maximum(jnp.dot(xb, w1_ref[:, cols], preferred_element_type=F32), 0.0)
        y = y + jnp.dot((hid * hid).astype(BF16), w2_ref[cols, :], preferred_element_type=F32)
    return _layer_norm_rows(y, g, b)


def _norm_then_mlp(y_halves, g1, b1, w1_ref, w2_ref, g2, b2, o_ref, alpha):
    mids = [_layer_norm_rows(y, g1, b1) for y in y_halves]
    row0 = 0
    for mid in mids:
        o_ref[row0:row0 + mid.shape[0], :] = _mlp_rows(mid, w1_ref, w2_ref, g2, b2, alpha)
        row0 += mid.shape[0]


def _row_halves(rows):
    half = rows // OUT_MLP_SPLIT
    return [slice(i * half, (i + 1) * half) for i in range(OUT_MLP_SPLIT)]


def _hy_out_mlp_kernel(a_ref, m_ref, wa_ref, wm_ref, x_ref, g1_ref, b1_ref, w1_ref, w2_ref, g2_ref, b2_ref,
                       o_ref, *, alpha):
    ys = []
    for rows in _row_halves(x_ref.shape[0]):
        y = alpha * x_ref[rows, :]
        y = y + jnp.dot(a_ref[rows, :], wa_ref[...], preferred_element_type=F32)
        ys.append(y + jnp.dot(m_ref[rows, :], wm_ref[...], preferred_element_type=F32))
    _norm_then_mlp(ys, g1_ref[...], b1_ref[...], w1_ref, w2_ref, g2_ref[...], b2_ref[...], o_ref, alpha)


def _hy_out_mlp(a2, m2, w_out, x2, g1, b1, w1, w2, g2, b2, alpha):
    tokens, d = x2.shape
    wa = a2.shape[1]
    wm = m2.shape[1]
    dff = w1.shape[1]
    t = OUT_MLP_TILE
    assert tokens % t == 0
    row = lambda width: pl.BlockSpec((t, width), lambda i: (i, 0))
    vec = _const_spec((1, d))
    return pl.pallas_call(
        functools.partial(_hy_out_mlp_kernel, alpha=alpha),
        grid=(tokens // t,),
        in_specs=[row(wa), row(wm), _const_spec((wa, d)), _const_spec((wm, d)), row(d), vec, vec,
                  _const_spec((d, dff)), _const_spec((dff, d)), vec, vec],
        out_specs=row(d),
        out_shape=jax.ShapeDtypeStruct((tokens, d), F32),
        compiler_params=_params("arbitrary"),
        name="hy_out_mlp",
    )(a2, m2, w_out[:wa], w_out[wa:], x2, g1.reshape(1, d), b1.reshape(1, d), w1, w2,
      g2.reshape(1, d), b2.reshape(1, d))


def _gla_out_mlp_kernel(o_in_ref, r_ref, w_ref, x_ref, g1_ref, b1_ref, w1_ref, w2_ref, g2_ref, b2_ref,
                        o_ref, *, alpha):
    ys = []
    for rows in _row_halves(x_ref.shape[0]):
        r = r_ref[rows, :].astype(F32)
        gated = o_in_ref[rows, :].astype(F32) * (r * jax.nn.sigmoid(r))
        ys.append(alpha * x_ref[rows, :] + jnp.dot(gated.astype(BF16), w_ref[...], preferred_element_type=F32))
    _norm_then_mlp(ys, g1_ref[...], b1_ref[...], w1_ref, w2_ref, g2_ref[...], b2_ref[...], o_ref, alpha)


def _gla_out_mlp(o2, h2, w_out, x2, g1, b1, w1, w2, g2, b2, alpha):
    tokens, d = x2.shape
    vw = o2.shape[1]
    dff = w1.shape[1]
    t = OUT_MLP_TILE
    r_block = (h2.shape[1] - vw) // vw
    assert r_block * vw + vw == h2.shape[1] and tokens % t == 0
    row = lambda width: pl.BlockSpec((t, width), lambda i: (i, 0))
    vec = _const_spec((1, d))
    return pl.pallas_call(
        functools.partial(_gla_out_mlp_kernel, alpha=alpha),
        grid=(tokens // t,),
        in_specs=[row(vw), pl.BlockSpec((t, vw), lambda i: (i, r_block)),
                  _const_spec((vw, d)), row(d), vec, vec,
                  _const_spec((d, dff)), _const_spec((dff, d)), vec, vec],
        out_specs=row(d),
        out_shape=jax.ShapeDtypeStruct((tokens, d), F32),
        compiler_params=_params("arbitrary"),
        name="gla_out_mlp",
    )(o2, h2, w_out, x2, g1.reshape(1, d), b1.reshape(1, d), w1, w2, g2.reshape(1, d), b2.reshape(1, d))


def _gla_inproj_kernel(x_ref, w_ref, wd_ref, wu_ref, bg_ref, h_ref, g_ref):
    xb = x_ref[...].astype(BF16)
    low = jnp.dot(xb, wd_ref[...], preferred_element_type=F32)
    z = jnp.dot(low.astype(BF16), wu_ref[...], preferred_element_type=F32) + bg_ref[...]
    log_sig = jnp.minimum(z, 0.0) - jnp.log1p(jnp.exp(-jnp.abs(z)))
    g_ref[...] = log_sig / GLA_GATE_NORM
    for c in range(w_ref.shape[1] // COL_CHUNK):
        cols = slice(c * COL_CHUNK, (c + 1) * COL_CHUNK)
        h_ref[:, cols] = jnp.dot(xb, w_ref[:, cols], preferred_element_type=F32).astype(BF16)


def _gla_inproj(x2, w_main, w_down, w_up, b_gate):
    tokens, d = x2.shape
    width = w_main.shape[1]
    kw = w_up.shape[1]
    row = lambda w_: pl.BlockSpec((ROW_TILE, w_), lambda i: (i, 0))
    return pl.pallas_call(
        _gla_inproj_kernel,
        grid=(tokens // ROW_TILE,),
        in_specs=[row(d), _const_spec((d, width)), _const_spec((d, LANES)),
                  _const_spec((LANES, kw)), _const_spec((1, kw))],
        out_specs=[row(width), row(kw)],
        out_shape=[jax.ShapeDtypeStruct((tokens, width), BF16),
                   jax.ShapeDtypeStruct((tokens, kw), F32)],
        compiler_params=_params("arbitrary"),
        name="gla_inproj",
    )(x2, w_main, w_down, w_up, b_gate.reshape(1, kw))


def _gla_level_ref(row_of, dk, half, rows):
    span = 2 * half
    pieces = []
    if span >= SUBLANES:
        for p in range(rows // span):
            r = p * span + half - 1
            pieces.append(jnp.broadcast_to(row_of(r), (span, dk)))
    else:
        sub = lax.broadcasted_iota(jnp.int32, (SUBLANES, dk), 0)
        for base in range(0, rows, SUBLANES):
            piece = None
            for p in range(SUBLANES // span):
                r = base + p * span + half - 1
                bc = jnp.broadcast_to(row_of(r), (SUBLANES, dk))
                piece = bc if piece is None else jnp.where(sub >= p * span, bc, piece)
            pieces.append(piece)
    return jnp.concatenate(pieces, axis=0)


def _gla_kernel(q_ref, k_ref, v_ref, g_ref, ng_ref, o_ref, st_ref, b_scr, *, scale):
    L = GLA_CHUNK
    dk, dv, heads = GLA_DK, GLA_DV, GLA_HEADS
    n_chunks = q_ref.shape[1] // L
    levels = [L >> (i + 1) for i in range(L.bit_length() - 1)]

    @pl.when(pl.program_id(1) == 0)
    def _reset():
        st_ref[...] = jnp.zeros(st_ref.shape, F32)

    r_i = lax.broadcasted_iota(jnp.int32, (L, L), 0)
    c_i = lax.broadcasted_iota(jnp.int32, (L, L), 1)
    tri = (c_i <= r_i).astype(BF16)
    level_of = jnp.where(c_i == r_i, -1, -2)
    for idx, half in enumerate(levels):
        span = 2 * half
        member = ((r_i // span) == (c_i // span)) & ((r_i % span) >= half) & ((c_i % span) < half)
        level_of = jnp.where(member, idx, level_of)

    def chunk_group(gi, carry):
        hs = range(heads)
        cs = range(GLA_GROUP)
        kcols = [slice(h * dk, (h + 1) * dk) for h in hs]
        vcols = [slice(h * dv, (h + 1) * dv) for h in hs]
        rows = [pl.ds(pl.multiple_of((gi * GLA_GROUP + c) * L, L), L) for c in cs]
        b = {}
        for c in cs:
            for h in hs:
                cs3 = jnp.dot(tri, jnp.concatenate(_split3(g_ref[0, rows[c], kcols[h]]), axis=1),
                              preferred_element_type=F32)
                b[c, h] = cs3[:, 0:dk] + cs3[:, dk:2 * dk] + cs3[:, 2 * dk:3 * dk]
                b_scr[c, h] = b[c, h]
        qb = {(c, h): q_ref[0, rows[c], kcols[h]] for c in cs for h in hs}
        kb = {(c, h): k_ref[0, rows[c], kcols[h]] for c in cs for h in hs}
        v = {(c, h): v_ref[0, rows[c], vcols[h]] for c in cs for h in hs}
        qf = {key: qb[key].astype(F32) * scale for key in qb}
        kf = {key: kb[key].astype(F32) for key in kb}

        att = {key: jnp.where(level_of == -1, _nt_dot(qb[key], kb[key]) * scale, 0.0) for key in qb}
        for idx, half in enumerate(levels):
            for c in cs:
                for h in hs:
                    ref = _gla_level_ref(lambda r, c=c, h=h: b_scr[c, h, r:r + 1, :], dk, half, L)
                    e = jnp.exp2(jnp.abs(b[c, h] - ref) * (-LOG2E))
                    a = _nt_dot((qf[c, h] * e).astype(BF16), (kf[c, h] * e).astype(BF16))
                    att[c, h] = jnp.where(level_of == idx, a, att[c, h])

        state = [st_ref[h] for h in hs]
        o = {}
        for c in cs:
            for h in hs:
                o[c, h] = jnp.dot((qf[c, h] * jnp.exp(b[c, h])).astype(BF16), state[h].astype(BF16),
                                  preferred_element_type=F32)
            for h in hs:
                b_last = b_scr[c, h, L - 1:L, :]
                kd = (kf[c, h] * jnp.exp(b_last - b[c, h])).astype(BF16)
                decay = jnp.transpose(jnp.broadcast_to(jnp.exp(b_last), (dk, dk)))
                decay = jnp.concatenate([decay] * (dv // dk), axis=1)
                state[h] = decay * state[h] + _tn_dot(kd, v[c, h])
        for h in hs:
            st_ref[h] = state[h]

        for c in cs:
            for h in hs:
                oh = o[c, h] + jnp.dot(att[c, h].astype(BF16), v[c, h], preferred_element_type=F32)
                y = oh * lax.rsqrt(jnp.mean(oh * oh, axis=-1, keepdims=True) + RMS_EPS)
                o_ref[0, rows[c], vcols[h]] = (y * ng_ref[...]).astype(o_ref.dtype)
        return carry

    lax.fori_loop(0, n_chunks // GLA_GROUP, chunk_group, 0)


def _gla_recurrence(h3, g3, norm_g):
    bsz, seq, _ = h3.shape
    t = GLA_STEP
    assert seq % t == 0 and t % (GLA_CHUNK * GLA_GROUP) == 0 and GLA_DV % GLA_DK == 0
    kw = GLA_HEADS * GLA_DK
    vw = GLA_HEADS * GLA_DV
    assert (2 * kw) % vw == 0
    return pl.pallas_call(
        functools.partial(_gla_kernel, scale=GLA_DK ** -0.5),
        grid=(bsz, seq // t),
        in_specs=[
            pl.BlockSpec((1, t, kw), lambda b, s: (b, s, 0)),
            pl.BlockSpec((1, t, kw), lambda b, s: (b, s, 1)),
            pl.BlockSpec((1, t, vw), lambda b, s: (b, s, (2 * kw) // vw)),
            pl.BlockSpec((1, t, kw), lambda b, s: (b, s, 0)),
            pl.BlockSpec((1, GLA_DV), lambda b, s: (0, 0)),
        ],
        out_specs=pl.BlockSpec((1, t, vw), lambda b, s: (b, s, 0)),
        out_shape=jax.ShapeDtypeStruct((bsz, seq, vw), BF16),
        scratch_shapes=[pltpu.VMEM((GLA_HEADS, GLA_DK, GLA_DV), F32),
                        pltpu.VMEM((GLA_GROUP, GLA_HEADS, GLA_CHUNK, GLA_DK), F32)],
        compiler_params=_params("arbitrary", "arbitrary"),
        name="gla_recurrence",
    )(h3, h3, h3, g3, norm_g.reshape(1, GLA_DV))


def kernel(x, hy_w_in, diff_lambda, diff_subln, hy_w_out, gla_w_in, gla_w_gate_up, gla_b_gate,
           gla_norm, gla_w_out, ln_mix_g, ln_mix_b, ffn_w1, ffn_w2, ln_ffn_g, ln_ffn_b):
    bsz, seq, d = x.shape
    depth = ln_mix_g.shape[0]
    alpha = (2 * depth) ** 0.25
    tokens = bsz * seq
    assert tokens % ROW_TILE == 0

    tabs = _rope_tables(seq, DIFF_QK_DIM) + _rope_tables(seq, MOBA_HEAD_DIM)
    gla_main = 2 * GLA_HEADS * GLA_DK + 2 * GLA_HEADS * GLA_DV

    x2 = x.reshape(tokens, d)
    for layer in range(depth):
        w1 = ffn_w1[layer].astype(BF16)
        w2 = ffn_w2[layer].astype(BF16)
        if layer % 2 == 0:
            e = layer // 2
            lambda_init = 0.8 - 0.6 * math.exp(-0.3 * layer)
            h2, kmean = _hy_inproj(x2, hy_w_in[e].astype(BF16), tabs, seq)
            h3 = h2.reshape(bsz, seq, h2.shape[1])
            kmean = kmean.reshape(bsz, seq // MOBA_BLOCK, kmean.shape[-1])
            a = _diff_attention(h3, diff_lambda[e], diff_subln[e], lambda_init)
            m = _moba_attention(h3, kmean)
            x2 = _hy_out_mlp(a.reshape(tokens, -1), m.reshape(tokens, -1), hy_w_out[e].astype(BF16), x2,
                             ln_mix_g[layer], ln_mix_b[layer], w1, w2, ln_ffn_g[layer], ln_ffn_b[layer], alpha)
        else:
            o = layer // 2
            w_in = gla_w_in[o]
            w_down = jnp.pad(w_in[:, gla_main:], ((0, 0), (0, LANES - GLA_GATE_RANK))).astype(BF16)
            w_up = jnp.pad(gla_w_gate_up[o], ((0, LANES - GLA_GATE_RANK), (0, 0))).astype(BF16)
            h2, g2 = _gla_inproj(x2, w_in[:, :gla_main].astype(BF16), w_down, w_up, gla_b_gate[o])
            h3 = h2.reshape(bsz, seq, h2.shape[1])
            g3 = g2.reshape(bsz, seq, g2.shape[1])
            og = _gla_recurrence(h3, g3, gla_norm[o])
            x2 = _gla_out_mlp(og.reshape(tokens, -1), h2, gla_w_out[o].astype(BF16), x2,
                              ln_mix_g[layer], ln_mix_b[layer], w1, w2, ln_ffn_g[layer], ln_ffn_b[layer], alpha)
    return x2.reshape(bsz, seq, d)
```

```python
import functools
import math

import jax
import jax.numpy as jnp
import numpy as np
from jax import lax
from jax.experimental import pallas as pl
from jax.experimental.pallas import tpu as pltpu

F32 = jnp.float32
BF16 = jnp.bfloat16

DIFF_HEADS = 4
DIFF_QK_DIM = 64
DIFF_V_DIM = 128
MOBA_HEADS = 4
MOBA_HEAD_DIM = 128
MOBA_BLOCK = 256
MOBA_TOPK = 3
GLA_HEADS = 4
GLA_DK = 128
GLA_DV = 256
GLA_GATE_RANK = 16
GLA_GATE_NORM = 16.0
ROPE_THETA = 10000.0
LN_EPS = 1e-5
RMS_EPS = 1e-5

LANES = 128
SUBLANES = 8
VMEM_LIMIT_BYTES = 56 * 1024 * 1024

ROW_TILE = 1024
COL_CHUNK = 512
OUT_MLP_TILE = 1024
OUT_MLP_SPLIT = 2
ATTN_TILE = 512
DIFF_KEY_TILE = 1024
MOBA_KEY_TILE = 2048
ATTN_ROWS = 128
ATTN_HEADS_PER_STEP = 4
LOG2E = math.log2(math.e)
GLA_CHUNK = 64
GLA_GROUP = 4
GLA_STEP = 512
NEG_BIG = -1e30


def _nt_dot(a, b, precision=None):
    return lax.dot_general(a, b, (((1,), (1,)), ((), ())),
                           preferred_element_type=F32, precision=precision)


def _tn_dot(a, b, precision=None):
    return lax.dot_general(a, b, (((0,), (0,)), ((), ())),
                           preferred_element_type=F32, precision=precision)


def _split3(x):
    x1 = x.astype(BF16)
    r1 = x - x1.astype(F32)
    x2 = r1.astype(BF16)
    x3 = (r1 - x2.astype(F32)).astype(BF16)
    return x1, x2, x3


def _const_spec(shape):
    nd = len(shape)
    return pl.BlockSpec(shape, lambda *_: (0,) * nd, pipeline_mode=pl.Buffered(1))


def _params(*sem):
    return pltpu.CompilerParams(dimension_semantics=sem, vmem_limit_bytes=VMEM_LIMIT_BYTES)


def _layer_norm_rows(y, g, b):
    mu = jnp.mean(y, axis=-1, keepdims=True)
    d = y - mu
    var = jnp.mean(d * d, axis=-1, keepdims=True)
    return d * lax.rsqrt(var + LN_EPS) * g + b


def _rope_tables(seq, group):
    half = group // 2
    inv = 1.0 / (ROPE_THETA ** (jnp.arange(0, group, 2, dtype=F32) / group))
    ang = jnp.arange(seq, dtype=F32)[:, None] * inv[None, :]
    cos, sin = jnp.cos(ang), jnp.sin(ang)
    cos_g = jnp.concatenate([cos, cos], axis=-1)
    sin_g = jnp.concatenate([-sin, sin], axis=-1)
    reps = LANES // group
    return jnp.tile(cos_g, (1, reps)), jnp.tile(sin_g, (1, reps))


def _rope(acc, cos, sin, group):
    width = acc.shape[1]
    half = group // 2
    lane = lax.broadcasted_iota(jnp.int32, acc.shape, 1)
    upper = pltpu.roll(acc, width - half, axis=1)
    lower = pltpu.roll(acc, half, axis=1)
    partner = jnp.where((lane % group) < half, upper, lower)
    reps = width // LANES
    c = jnp.concatenate([cos] * reps, axis=1)
    s = jnp.concatenate([sin] * reps, axis=1)
    return acc * c + partner * s


def _cast_weights_once(w_ref, wb_scr):
    @pl.when(pl.program_id(0) == 0)
    def _cast():
        for c in range(wb_scr.shape[1] // COL_CHUNK):
            cols = slice(c * COL_CHUNK, (c + 1) * COL_CHUNK)
            wb_scr[:, cols] = w_ref[:, cols].astype(BF16)


def _hy_inproj_kernel(x_ref, w_ref, cd_ref, sd_ref, cm_ref, sm_ref, h_ref, km_ref, wb_scr):
    _cast_weights_once(w_ref, wb_scr)
    xb = x_ref[...].astype(BF16)
    rows = xb.shape[0]
    n_chunks = wb_scr.shape[1] // COL_CHUNK
    for c in range(n_chunks):
        cols = slice(c * COL_CHUNK, (c + 1) * COL_CHUNK)
        acc = jnp.dot(xb, wb_scr[:, cols], preferred_element_type=F32)
        if c == 0:
            acc = _rope(acc, cd_ref[...], sd_ref[...], DIFF_QK_DIM) * (DIFF_QK_DIM ** -0.5 * LOG2E)
        elif c == 1:
            acc = _rope(acc, cd_ref[...], sd_ref[...], DIFF_QK_DIM)
        elif c == 3:
            acc = _rope(acc, cm_ref[...], sm_ref[...], MOBA_HEAD_DIM) * (MOBA_HEAD_DIM ** -0.5 * LOG2E)
        elif c == 4:
            acc = _rope(acc, cm_ref[...], sm_ref[...], MOBA_HEAD_DIM)
        if c == 4:
            blocks = rows // MOBA_BLOCK
            km_ref[0] = jnp.mean(acc.reshape(blocks, MOBA_BLOCK, COL_CHUNK), axis=1)
        h_ref[:, cols] = acc.astype(BF16)


def _hy_inproj(x2, w, tabs, seq):
    tokens, d = x2.shape
    width = w.shape[1]
    assert width == 6 * COL_CHUNK and seq % ROW_TILE == 0 and ROW_TILE % MOBA_BLOCK == 0
    steps = tokens // ROW_TILE
    per_seq = seq // ROW_TILE
    blocks = ROW_TILE // MOBA_BLOCK
    tab_spec = pl.BlockSpec((ROW_TILE, LANES), lambda i: (i % per_seq, 0))
    return pl.pallas_call(
        _hy_inproj_kernel,
        grid=(steps,),
        in_specs=[pl.BlockSpec((ROW_TILE, d), lambda i: (i, 0)),
                  _const_spec((d, width)),
                  tab_spec, tab_spec, tab_spec, tab_spec],
        out_specs=[pl.BlockSpec((ROW_TILE, width), lambda i: (i, 0)),
                   pl.BlockSpec((1, blocks, COL_CHUNK), lambda i: (i, 0, 0))],
        out_shape=[jax.ShapeDtypeStruct((tokens, width), BF16),
                   jax.ShapeDtypeStruct((steps, blocks, COL_CHUNK), F32)],
        scratch_shapes=[pltpu.VMEM((d, width), BF16)],
        compiler_params=_params("arbitrary"),
        name="hy_inproj",
    )(x2, w, *tabs)


def _causal_pairs(n_q, ratio):
    qi = np.array([q for q in range(n_q) for _ in range(q // ratio + 1)], np.int32)
    ki = np.array([k for q in range(n_q) for k in range(q // ratio + 1)], np.int32)
    return jnp.asarray(qi), jnp.asarray(ki)


def _for_each_tile_kind(qi, ki, tq, tk, step, finish):
    ratio = tk // tq

    @pl.when(ki < qi // ratio)
    def _full():
        step(None)

    for part in range(ratio):
        @pl.when((ki == qi // ratio) & (qi % ratio == part))
        def _diag(part=part):
            step(part * tq)
            finish()


def _with_ones(v):
    return jnp.concatenate([v, jnp.ones(v.shape, v.dtype)], axis=1)


def _flash_update(s, v_ext, m_ref, acc_ref, rows):
    reps = s.shape[1] // LANES
    m_prev = m_ref[rows, :]
    m_new = jnp.maximum(m_prev, jnp.max(s, axis=-1, keepdims=True))
    alpha = jnp.exp2(m_prev - m_new)
    p = jnp.exp2(s - jnp.concatenate([m_new] * reps, axis=1))
    pv = jnp.dot(p.astype(BF16), v_ext, preferred_element_type=F32)
    acc_ref[rows, :] = jnp.concatenate([alpha, alpha], axis=1) * acc_ref[rows, :] + pv
    m_ref[rows, :] = m_new


def _attend_tile(s_all, v, m_ref, acc_ref, row0, tq, diag_offset):
    rb = ATTN_ROWS
    v_ext = _with_ones(v)
    for r in range(s_all.shape[0] // rb):
        rows = slice(row0 + r * rb, row0 + (r + 1) * rb)
        s = s_all[r * rb:(r + 1) * rb]
        if diag_offset is not None:
            first = diag_offset + (r * rb) % tq
            keys = first + rb
            row = lax.broadcasted_iota(jnp.int32, (rb, keys), 0) + first
            col = lax.broadcasted_iota(jnp.int32, (rb, keys), 1)
            s = jnp.where(col <= row, s[:, 0:keys], NEG_BIG)
            _flash_update(s, v_ext[0:keys], m_ref, acc_ref, rows)
        else:
            _flash_update(s, v_ext, m_ref, acc_ref, rows)


def _diff_attn_kernel(qi_tab, ki_tab, q_ref, k_ref, v_ref, lam_ref, sub_ref, o_ref,
                      qs_scr, m_scr, acc_scr, *, lambda_init):
    p_id = pl.program_id(2)
    qi = qi_tab[p_id]
    ki = ki_tab[p_id]
    tq = q_ref.shape[1]
    heads = q_ref.shape[2] // LANES
    dv = DIFF_V_DIM

    @pl.when(ki == 0)
    def _init():
        for hh in range(heads):
            q = q_ref[0, :, hh * LANES:(hh + 1) * LANES]
            lane = lax.broadcasted_iota(jnp.int32, q.shape, 1)
            zero = jnp.zeros_like(q)
            base = hh * 2 * tq
            qs_scr[base:base + tq, :] = jnp.where(lane < DIFF_QK_DIM, q, zero)
            qs_scr[base + tq:base + 2 * tq, :] = jnp.where(lane >= DIFF_QK_DIM, q, zero)
        m_scr[...] = jnp.full(m_scr.shape, NEG_BIG, F32)
        acc_scr[...] = jnp.zeros(acc_scr.shape, F32)

    def step(diag_offset):
        scores = [_nt_dot(qs_scr[hh * 2 * tq:(hh + 1) * 2 * tq, :], k_ref[0, :, hh * LANES:(hh + 1) * LANES])
                  for hh in range(heads)]
        for hh in range(heads):
            _attend_tile(scores[hh], v_ref[0, :, hh * dv:(hh + 1) * dv], m_scr, acc_scr, hh * 2 * tq, tq,
                         diag_offset)

    def finish():
        lam = lam_ref[...].astype(F32)
        lam_full = (jnp.exp(jnp.sum(lam[0:1] * lam[1:2], axis=-1, keepdims=True))
                    - jnp.exp(jnp.sum(lam[2:3] * lam[3:4], axis=-1, keepdims=True))
                    + lambda_init)
        for hh in range(heads):
            base = hh * 2 * tq
            o1 = acc_scr[base:base + tq, 0:dv] / acc_scr[base:base + tq, dv:]
            o2 = acc_scr[base + tq:base + 2 * tq, 0:dv] / acc_scr[base + tq:base + 2 * tq, dv:]
            o = o1 - lam_full * o2
            y = o * lax.rsqrt(jnp.mean(o * o, axis=-1, keepdims=True) + RMS_EPS)
            o_ref[0, :, hh * dv:(hh + 1) * dv] = (y * sub_ref[...] * (1.0 - lambda_init)).astype(o_ref.dtype)

    _for_each_tile_kind(qi, ki, tq, k_ref.shape[1], step, finish)


def _diff_attention(h3, lam, subln, lambda_init):
    bsz, seq, _ = h3.shape
    t = ATTN_TILE
    tk = DIFF_KEY_TILE
    assert seq % tk == 0 and tk % t == 0
    qi_tab, ki_tab = _causal_pairs(seq // t, tk // t)
    hp = ATTN_HEADS_PER_STEP
    assert DIFF_HEADS % hp == 0 and 2 * DIFF_QK_DIM == LANES and DIFF_V_DIM == LANES
    groups = DIFF_HEADS // hp
    w = hp * LANES
    grid_spec = pltpu.PrefetchScalarGridSpec(
        num_scalar_prefetch=2,
        grid=(bsz, groups, int(qi_tab.shape[0])),
        in_specs=[
            pl.BlockSpec((1, t, w), lambda b, h, p, qt, kt: (b, qt[p], h)),
            pl.BlockSpec((1, tk, w), lambda b, h, p, qt, kt: (b, kt[p], groups + h)),
            pl.BlockSpec((1, tk, w), lambda b, h, p, qt, kt: (b, kt[p], 2 * groups + h)),
            pl.BlockSpec((4, DIFF_QK_DIM), lambda b, h, p, qt, kt: (0, 0)),
            pl.BlockSpec((1, DIFF_V_DIM), lambda b, h, p, qt, kt: (0, 0)),
        ],
        out_specs=pl.BlockSpec((1, t, w), lambda b, h, p, qt, kt: (b, qt[p], h)),
        scratch_shapes=[pltpu.VMEM((hp * 2 * t, LANES), BF16),
                        pltpu.VMEM((hp * 2 * t, LANES), F32),
                        pltpu.VMEM((hp * 2 * t, 2 * DIFF_V_DIM), F32)],
    )
    return pl.pallas_call(
        functools.partial(_diff_attn_kernel, lambda_init=lambda_init),
        grid_spec=grid_spec,
        out_shape=jax.ShapeDtypeStruct((bsz, seq, DIFF_HEADS * DIFF_V_DIM), BF16),
        compiler_params=_params("arbitrary", "arbitrary", "arbitrary"),
        name="diff_attn",
    )(qi_tab, ki_tab, h3, h3, h3, lam, subln.reshape(1, DIFF_V_DIM))


def _moba_kernel(qi_tab, ki_tab, q_ref, k_ref, v_ref, km_ref, oh_ref, o_ref,
                 qa_scr, m_scr, acc_scr, *, topk):
    p_id = pl.program_id(2)
    qi = qi_tab[p_id]
    ki = ki_tab[p_id]
    tq = q_ref.shape[1]
    d = MOBA_HEAD_DIM
    heads = q_ref.shape[2] // d

    @pl.when(ki == 0)
    def _init():
        nb = km_ref.shape[1]
        blk = lax.broadcasted_iota(jnp.int32, (nb, tq), 0)
        pos = lax.broadcasted_iota(jnp.int32, (nb, tq), 1)
        own = (qi * tq + pos) // MOBA_BLOCK
        for hh in range(heads):
            q = q_ref[0, :, hh * d:(hh + 1) * d]
            km3 = jnp.concatenate(_split3(km_ref[0, :, hh * d:(hh + 1) * d]), axis=0)
            g3 = _nt_dot(km3, q)
            gate = g3[0:nb] + g3[nb:2 * nb] + g3[2 * nb:3 * nb]
            g = jnp.where(blk < own, gate, -jnp.inf)
            sel = blk == own
            for _ in range(topk):
                mx = jnp.max(g, axis=0, keepdims=True)
                idx = jnp.min(jnp.where(g == mx, blk, nb), axis=0, keepdims=True)
                sel = sel | ((blk == idx) & (mx > -jnp.inf))
                g = jnp.where(blk == idx, -jnp.inf, g)
            bias_t = jnp.concatenate([jnp.where(sel, 0.0, NEG_BIG),
                                      jnp.full((LANES - nb, tq), NEG_BIG, F32)], axis=0)
            qa_scr[hh * tq:(hh + 1) * tq, 0:d] = q
            qa_scr[hh * tq:(hh + 1) * tq, d:2 * d] = jnp.transpose(bias_t).astype(BF16)
        m_scr[...] = jnp.full(m_scr.shape, NEG_BIG, F32)
        acc_scr[...] = jnp.zeros(acc_scr.shape, F32)

    def step(diag_offset):
        onehot = oh_ref[...]
        scores = [_nt_dot(qa_scr[hh * tq:(hh + 1) * tq, :],
                          jnp.concatenate([k_ref[0, :, hh * d:(hh + 1) * d], onehot], axis=1))
                  for hh in range(heads)]
        for hh in range(heads):
            _attend_tile(scores[hh], v_ref[0, :, hh * d:(hh + 1) * d], m_scr, acc_scr, hh * tq, tq, diag_offset)

    def finish():
        for hh in range(heads):
            rows = slice(hh * tq, (hh + 1) * tq)
            o_ref[0, :, hh * d:(hh + 1) * d] = (acc_scr[rows, 0:d] / acc_scr[rows, d:2 * d]).astype(o_ref.dtype)

    _for_each_tile_kind(qi, ki, tq, k_ref.shape[1], step, finish)


def _moba_attention(h3, kmean):
    bsz, seq, _ = h3.shape
    t = ATTN_TILE
    tk = MOBA_KEY_TILE
    nb = seq // MOBA_BLOCK
    assert seq % tk == 0 and tk % t == 0 and t % MOBA_BLOCK == 0 and nb <= LANES and MOBA_HEAD_DIM == LANES
    qi_tab, ki_tab = _causal_pairs(seq // t, tk // t)
    hp = ATTN_HEADS_PER_STEP
    assert MOBA_HEADS % hp == 0
    groups = MOBA_HEADS // hp
    w = hp * MOBA_HEAD_DIM
    q0 = (3 * DIFF_HEADS * DIFF_V_DIM) // w
    k0 = q0 + groups
    v0 = k0 + groups
    block_of_key = jnp.arange(seq, dtype=jnp.int32)[:, None] // MOBA_BLOCK
    onehot = (block_of_key == jnp.arange(LANES, dtype=jnp.int32)[None, :]).astype(BF16)
    grid_spec = pltpu.PrefetchScalarGridSpec(
        num_scalar_prefetch=2,
        grid=(bsz, groups, int(qi_tab.shape[0])),
        in_specs=[
            pl.BlockSpec((1, t, w), lambda b, h, p, qt, kt: (b, qt[p], q0 + h)),
            pl.BlockSpec((1, tk, w), lambda b, h, p, qt, kt: (b, kt[p], k0 + h)),
            pl.BlockSpec((1, tk, w), lambda b, h, p, qt, kt: (b, kt[p], v0 + h)),
            pl.BlockSpec((1, nb, w), lambda b, h, p, qt, kt: (b, 0, h)),
            pl.BlockSpec((tk, LANES), lambda b, h, p, qt, kt: (kt[p], 0)),
        ],
        out_specs=pl.BlockSpec((1, t, w), lambda b, h, p, qt, kt: (b, qt[p], h)),
        scratch_shapes=[pltpu.VMEM((hp * t, 2 * MOBA_HEAD_DIM), BF16),
                        pltpu.VMEM((hp * t, LANES), F32),
                        pltpu.VMEM((hp * t, 2 * MOBA_HEAD_DIM), F32)],
    )
    return pl.pallas_call(
        functools.partial(_moba_kernel, topk=min(MOBA_TOPK, nb)),
        grid_spec=grid_spec,
        out_shape=jax.ShapeDtypeStruct((bsz, seq, MOBA_HEADS * MOBA_HEAD_DIM), BF16),
        compiler_params=_params("arbitrary", "arbitrary", "arbitrary"),
        name="moba_attn",
    )(qi_tab, ki_tab, h3, h3, h3, kmean, onehot)


def _mlp_rows(x, w1_ref, w2_ref, g, b, alpha):
    xb = x.astype(BF16)
    y = alpha * x
    for c in range(w1_ref.shape[1] // COL_CHUNK):
        cols = slice(c * COL_CHUNK, (c + 1) * COL_CHUNK)
        hid = jnp.maximum(jnp.dot(xb, w1_ref[:, cols], preferred_element_type=F32), 0.0)
        y = y + jnp.dot((hid * hid).astype(BF16), w2_ref[cols, :], preferred_element_type=F32)
    return _layer_norm_rows(y, g, b)


def _norm_then_mlp(y_halves, g1, b1, w1_ref, w2_ref, g2, b2, o_ref, alpha):
    mids = [_layer_norm_rows(y, g1, b1) for y in y_halves]
    row0 = 0
    for mid in mids:
        o_ref[row0:row0 + mid.shape[0], :] = _mlp_rows(mid, w1_ref, w2_ref, g2, b2, alpha)
        row0 += mid.shape[0]


def _row_halves(rows):
    half = rows // OUT_MLP_SPLIT
    return [slice(i * half, (i + 1) * half) for i in range(OUT_MLP_SPLIT)]


def _hy_out_mlp_kernel(a_ref, m_ref, wa_ref, wm_ref, x_ref, g1_ref, b1_ref, w1_ref, w2_ref, g2_ref, b2_ref,
                       o_ref, *, alpha):
    ys = []
    for rows in _row_halves(x_ref.shape[0]):
        y = alpha * x_ref[rows, :]
        y = y + jnp.dot(a_ref[rows, :], wa_ref[...], preferred_element_type=F32)
        ys.append(y + jnp.dot(m_ref[rows, :], wm_ref[...], preferred_element_type=F32))
    _norm_then_mlp(ys, g1_ref[...], b1_ref[...], w1_ref, w2_ref, g2_ref[...], b2_ref[...], o_ref, alpha)


def _hy_out_mlp(a2, m2, w_out, x2, g1, b1, w1, w2, g2, b2, alpha):
    tokens, d = x2.shape
    wa = a2.shape[1]
    wm = m2.shape[1]
    dff = w1.shape[1]
    t = OUT_MLP_TILE
    assert tokens % t == 0
    row = lambda width: pl.BlockSpec((t, width), lambda i: (i, 0))
    vec = _const_spec((1, d))
    return pl.pallas_call(
        functools.partial(_hy_out_mlp_kernel, alpha=alpha),
        grid=(tokens // t,),
        in_specs=[row(wa), row(wm), _const_spec((wa, d)), _const_spec((wm, d)), row(d), vec, vec,
                  _const_spec((d, dff)), _const_spec((dff, d)), vec, vec],
        out_specs=row(d),
        out_shape=jax.ShapeDtypeStruct((tokens, d), F32),
        compiler_params=_params("arbitrary"),
        name="hy_out_mlp",
    )(a2, m2, w_out[:wa], w_out[wa:], x2, g1.reshape(1, d), b1.reshape(1, d), w1, w2,
      g2.reshape(1, d), b2.reshape(1, d))


def _gla_out_mlp_kernel(o_in_ref, r_ref, w_ref, x_ref, g1_ref, b1_ref, w1_ref, w2_ref, g2_ref, b2_ref,
                        o_ref, *, alpha):
    ys = []
    for rows in _row_halves(x_ref.shape[0]):
        r = r_ref[rows, :].astype(F32)
        gated = o_in_ref[rows, :].astype(F32) * (r * jax.nn.sigmoid(r))
        ys.append(alpha * x_ref[rows, :] + jnp.dot(gated.astype(BF16), w_ref[...], preferred_element_type=F32))
    _norm_then_mlp(ys, g1_ref[...], b1_ref[...], w1_ref, w2_ref, g2_ref[...], b2_ref[...], o_ref, alpha)


def _gla_out_mlp(o2, h2, w_out, x2, g1, b1, w1, w2, g2, b2, alpha):
    tokens, d = x2.shape
    vw = o2.shape[1]
    dff = w1.shape[1]
    t = OUT_MLP_TILE
    r_block = (h2.shape[1] - vw) // vw
    assert r_block * vw + vw == h2.shape[1] and tokens % t == 0
    row = lambda width: pl.BlockSpec((t, width), lambda i: (i, 0))
    vec = _const_spec((1, d))
    return pl.pallas_call(
        functools.partial(_gla_out_mlp_kernel, alpha=alpha),
        grid=(tokens // t,),
        in_specs=[row(vw), pl.BlockSpec((t, vw), lambda i: (i, r_block)),
                  _const_spec((vw, d)), row(d), vec, vec,
                  _const_spec((d, dff)), _const_spec((dff, d)), vec, vec],
        out_specs=row(d),
        out_shape=jax.ShapeDtypeStruct((tokens, d), F32),
        compiler_params=_params("arbitrary"),
        name="gla_out_mlp",
    )(o2, h2, w_out, x2, g1.reshape(1, d), b1.reshape(1, d), w1, w2, g2.reshape(1, d), b2.reshape(1, d))


def _gla_inproj_kernel(x_ref, w_ref, wd_ref, wu_ref, bg_ref, h_ref, g_ref, wb_scr):
    _cast_weights_once(w_ref, wb_scr)
    xb = x_ref[...].astype(BF16)
    low = jnp.dot(xb, wd_ref[...], preferred_element_type=F32)
    z = jnp.dot(low.astype(BF16), wu_ref[...], preferred_element_type=F32) + bg_ref[...]
    log_sig = jnp.minimum(z, 0.0) - jnp.log1p(jnp.exp(-jnp.abs(z)))
    g_ref[...] = log_sig / GLA_GATE_NORM
    for c in range(wb_scr.shape[1] // COL_CHUNK):
        cols = slice(c * COL_CHUNK, (c + 1) * COL_CHUNK)
        h_ref[:, cols] = jnp.dot(xb, wb_scr[:, cols], preferred_element_type=F32).astype(BF16)


def _gla_inproj(x2, w_in, width, w_down, w_up, b_gate):
    tokens, d = x2.shape
    kw = w_up.shape[1]
    assert width % COL_CHUNK == 0 and width <= w_in.shape[1]
    row = lambda w_: pl.BlockSpec((ROW_TILE, w_), lambda i: (i, 0))
    return pl.pallas_call(
        _gla_inproj_kernel,
        grid=(tokens // ROW_TILE,),
        in_specs=[row(d), _const_spec((d, width)), _const_spec((d, LANES)),
                  _const_spec((LANES, kw)), _const_spec((1, kw))],
        out_specs=[row(width), row(kw)],
        out_shape=[jax.ShapeDtypeStruct((tokens, width), BF16),
                   jax.ShapeDtypeStruct((tokens, kw), F32)],
        scratch_shapes=[pltpu.VMEM((d, width), BF16)],
        compiler_params=_params("arbitrary"),
        name="gla_inproj",
    )(x2, w_in, w_down, w_up, b_gate.reshape(1, kw))


def _gla_level_ref(row_of, dk, half, rows):
    span = 2 * half
    pieces = []
    if span >= SUBLANES:
        for p in range(rows // span):
            r = p * span + half - 1
            pieces.append(jnp.broadcast_to(row_of(r), (span, dk)))
    else:
        sub = lax.broadcasted_iota(jnp.int32, (SUBLANES, dk), 0)
        for base in range(0, rows, SUBLANES):
            piece = None
            for p in range(SUBLANES // span):
                r = base + p * span + half - 1
                bc = jnp.broadcast_to(row_of(r), (SUBLANES, dk))
                piece = bc if piece is None else jnp.where(sub >= p * span, bc, piece)
            pieces.append(piece)
    return jnp.concatenate(pieces, axis=0)


def _gla_kernel(q_ref, k_ref, v_ref, g_ref, ng_ref, o_ref, st_ref, b_scr, *, scale):
    L = GLA_CHUNK
    dk, dv, heads = GLA_DK, GLA_DV, GLA_HEADS
    n_chunks = q_ref.shape[1] // L
    levels = [L >> (i + 1) for i in range(L.bit_length() - 1)]

    @pl.when(pl.program_id(1) == 0)
    def _reset():
        st_ref[...] = jnp.zeros(st_ref.shape, F32)

    r_i = lax.broadcasted_iota(jnp.int32, (L, L), 0)
    c_i = lax.broadcasted_iota(jnp.int32, (L, L), 1)
    tri = (c_i <= r_i).astype(BF16)
    level_of = jnp.where(c_i == r_i, -1, -2)
    for idx, half in enumerate(levels):
        span = 2 * half
        member = ((r_i // span) == (c_i // span)) & ((r_i % span) >= half) & ((c_i % span) < half)
        level_of = jnp.where(member, idx, level_of)

    def chunk_group(gi, carry):
        hs = range(heads)
        cs = range(GLA_GROUP)
        kcols = [slice(h * dk, (h + 1) * dk) for h in hs]
        vcols = [slice(h * dv, (h + 1) * dv) for h in hs]
        rows = [pl.ds(pl.multiple_of((gi * GLA_GROUP + c) * L, L), L) for c in cs]
        b = {}
        for c in cs:
            for h in hs:
                cs3 = jnp.dot(tri, jnp.concatenate(_split3(g_ref[0, rows[c], kcols[h]]), axis=1),
                              preferred_element_type=F32)
                b[c, h] = cs3[:, 0:dk] + cs3[:, dk:2 * dk] + cs3[:, 2 * dk:3 * dk]
                b_scr[c, h] = b[c, h]
        qb = {(c, h): q_ref[0, rows[c], kcols[h]] for c in cs for h in hs}
        kb = {(c, h): k_ref[0, rows[c], kcols[h]] for c in cs for h in hs}
        v = {(c, h): v_ref[0, rows[c], vcols[h]] for c in cs for h in hs}
        qf = {key: qb[key].astype(F32) * scale for key in qb}
        kf = {key: kb[key].astype(F32) for key in kb}

        att = {key: jnp.where(level_of == -1, _nt_dot(qb[key], kb[key]) * scale, 0.0) for key in qb}
        for idx, half in enumerate(levels):
            for c in cs:
                for h in hs:
                    ref = _gla_level_ref(lambda r, c=c, h=h: b_scr[c, h, r:r + 1, :], dk, half, L)
                    e = jnp.exp2(jnp.abs(b[c, h] - ref) * (-LOG2E))
                    a = _nt_dot((qf[c, h] * e).astype(BF16), (kf[c, h] * e).astype(BF16))
                    att[c, h] = jnp.where(level_of == idx, a, att[c, h])

        state = [st_ref[h] for h in hs]
        o = {}
        for c in cs:
            for h in hs:
                o[c, h] = jnp.dot((qf[c, h] * jnp.exp(b[c, h])).astype(BF16), state[h].astype(BF16),
                                  preferred_element_type=F32)
            for h in hs:
                b_last = b_scr[c, h, L - 1:L, :]
                kd = (kf[c, h] * jnp.exp(b_last - b[c, h])).astype(BF16)
                decay = jnp.transpose(jnp.broadcast_to(jnp.exp(b_last), (dk, dk)))
                decay = jnp.concatenate([decay] * (dv // dk), axis=1)
                state[h] = decay * state[h] + _tn_dot(kd, v[c, h])
        for h in hs:
            st_ref[h] = state[h]

        for c in cs:
            for h in hs:
                oh = o[c, h] + jnp.dot(att[c, h].astype(BF16), v[c, h], preferred_element_type=F32)
                y = oh * lax.rsqrt(jnp.mean(oh * oh, axis=-1, keepdims=True) + RMS_EPS)
                o_ref[0, rows[c], vcols[h]] = (y * ng_ref[...]).astype(o_ref.dtype)
        return carry

    lax.fori_loop(0, n_chunks // GLA_GROUP, chunk_group, 0)


def _gla_recurrence(h3, g3, norm_g):
    bsz, seq, _ = h3.shape
    t = GLA_STEP
    assert seq % t == 0 and t % (GLA_CHUNK * GLA_GROUP) == 0 and GLA_DV % GLA_DK == 0
    kw = GLA_HEADS * GLA_DK
    vw = GLA_HEADS * GLA_DV
    assert (2 * kw) % vw == 0
    return pl.pallas_call(
        functools.partial(_gla_kernel, scale=GLA_DK ** -0.5),
        grid=(bsz, seq // t),
        in_specs=[
            pl.BlockSpec((1, t, kw), lambda b, s: (b, s, 0)),
            pl.BlockSpec((1, t, kw), lambda b, s: (b, s, 1)),
            pl.BlockSpec((1, t, vw), lambda b, s: (b, s, (2 * kw) // vw)),
            pl.BlockSpec((1, t, kw), lambda b, s: (b, s, 0)),
            pl.BlockSpec((1, GLA_DV), lambda b, s: (0, 0)),
        ],
        out_specs=pl.BlockSpec((1, t, vw), lambda b, s: (b, s, 0)),
        out_shape=jax.ShapeDtypeStruct((bsz, seq, vw), BF16),
        scratch_shapes=[pltpu.VMEM((GLA_HEADS, GLA_DK, GLA_DV), F32),
                        pltpu.VMEM((GLA_GROUP, GLA_HEADS, GLA_CHUNK, GLA_DK), F32)],
        compiler_params=_params("arbitrary", "arbitrary"),
        name="gla_recurrence",
    )(h3, h3, h3, g3, norm_g.reshape(1, GLA_DV))


def kernel(x, hy_w_in, diff_lambda, diff_subln, hy_w_out, gla_w_in, gla_w_gate_up, gla_b_gate,
           gla_norm, gla_w_out, ln_mix_g, ln_mix_b, ffn_w1, ffn_w2, ln_ffn_g, ln_ffn_b):
    bsz, seq, d = x.shape
    depth = ln_mix_g.shape[0]
    alpha = (2 * depth) ** 0.25
    tokens = bsz * seq
    assert tokens % ROW_TILE == 0

    tabs = _rope_tables(seq, DIFF_QK_DIM) + _rope_tables(seq, MOBA_HEAD_DIM)
    gla_main = 2 * GLA_HEADS * GLA_DK + 2 * GLA_HEADS * GLA_DV

    x2 = x.reshape(tokens, d)
    for layer in range(depth):
        w1 = ffn_w1[layer].astype(BF16)
        w2 = ffn_w2[layer].astype(BF16)
        if layer % 2 == 0:
            e = layer // 2
            lambda_init = 0.8 - 0.6 * math.exp(-0.3 * layer)
            h2, kmean = _hy_inproj(x2, hy_w_in[e], tabs, seq)
            h3 = h2.reshape(bsz, seq, h2.shape[1])
            kmean = kmean.reshape(bsz, seq // MOBA_BLOCK, kmean.shape[-1])
            a = _diff_attention(h3, diff_lambda[e], diff_subln[e], lambda_init)
            m = _moba_attention(h3, kmean)
            x2 = _hy_out_mlp(a.reshape(tokens, -1), m.reshape(tokens, -1), hy_w_out[e].astype(BF16), x2,
                             ln_mix_g[layer], ln_mix_b[layer], w1, w2, ln_ffn_g[layer], ln_ffn_b[layer], alpha)
        else:
            o = layer // 2
            w_in = gla_w_in[o]
            w_down = jnp.pad(w_in[:, gla_main:], ((0, 0), (0, LANES - GLA_GATE_RANK))).astype(BF16)
            w_up = jnp.pad(gla_w_gate_up[o], ((0, LANES - GLA_GATE_RANK), (0, 0))).astype(BF16)
            h2, g2 = _gla_inproj(x2, w_in, gla_main, w_down, w_up, gla_b_gate[o])
            h3 = h2.reshape(bsz, seq, h2.shape[1])
            g3 = g2.reshape(bsz, seq, g2.shape[1])
            og = _gla_recurrence(h3, g3, gla_norm[o])
            x2 = _gla_out_mlp(og.reshape(tokens, -1), h2, gla_w_out[o].astype(BF16), x2,
                              ln_mix_g[layer], ln_mix_b[layer], w1, w2, ln_ffn_g[layer], ln_ffn_b[layer], alpha)
    return x2.reshape(bsz, seq, d)
```
